```python
import math
import jax, jax.numpy as jnp
from jax import lax
import numpy as np

D_MODEL = 1024
BATCH = 4
SEQ = 4096
DEPTH = 1
DEC_BATCH = 32
DEC_SEQ = 8
PAST_LEN = 8192
PAGE_SIZE = 128

MIX_WIDTH = D_MODEL
HEAD_DIM = 64
ATTN_WIDTH = MIX_WIDTH // 2
CONV_WIDTH = MIX_WIDTH - ATTN_WIDTH
SB_HEADS = ATTN_WIDTH // HEAD_DIM
CONV_K = 3
Q_BLOCK = 128
N_EXPERTS = 32
TOP_K = 4
D_FF = D_MODEL
SWIGLU_LIMIT = 7.0
SWIGLU_ALPHA = 1.702
PLE_DIM = 256
RMS_EPS = 1e-6
SB_BIAS_MIN = 4.0
SB_BIAS_MAX = 10.0
IN_SPLITS = (ATTN_WIDTH, 2 * ATTN_WIDTH, 3 * ATTN_WIDTH, 3 * ATTN_WIDTH + CONV_WIDTH, 3 * ATTN_WIDTH + 2 * CONV_WIDTH)
IN_WIDTH = 3 * ATTN_WIDTH + 3 * CONV_WIDTH

kernel_name = 'hybrid_stickbreak_shortconv_moe_ple_step'


def rms_norm(x, g):
    xf = x.astype(jnp.float32)
    y = xf * lax.rsqrt(jnp.mean(xf * xf, axis=-1, keepdims=True) + RMS_EPS)
    return (y * g.astype(jnp.float32)).astype(x.dtype)


def stick_breaking(q, segments, q_pos0, sb_bias):
    b, t, h, dh = q.shape
    blk = min(Q_BLOCK, t)
    n_blk = -(-t // blk)
    qp = jnp.pad(q, ((0, 0), (0, n_blk * blk - t), (0, 0), (0, 0)))
    qp = qp.reshape(b, n_blk, blk, h, dh).transpose(1, 0, 2, 3, 4)
    k_pos = jnp.concatenate([pos for _, _, pos in segments])
    sizes = [kk.shape[1] for kk, _, _ in segments]
    scale = 1.0 / math.sqrt(dh)
    bias = sb_bias.astype(jnp.float32)[None, :, None, None]

    def one_block(args):
        q_b, i = args
        q_pos = q_pos0 + i * blk + jnp.arange(blk)
        z = jnp.concatenate(
            [jnp.einsum('bthd,bshd->bhts', q_b, kk, preferred_element_type=jnp.float32) for kk, _, _ in segments],
            axis=-1) * scale + bias
        mask = k_pos[None, :] < q_pos[:, None]
        log_beta = jax.nn.log_sigmoid(z)
        log_keep = jnp.where(mask, jax.nn.log_sigmoid(-z), 0.0)
        later = lax.cumsum(log_keep, axis=3, reverse=True) - log_keep
        w = jnp.where(mask, jnp.exp(log_beta + later), 0.0)
        out = jnp.zeros((b, blk, h, dh), jnp.float32)
        off = 0
        for (_, vv, _), n in zip(segments, sizes):
            out = out + jnp.einsum('bhts,bshd->bthd', w[..., off:off + n].astype(vv.dtype), vv,
                                   preferred_element_type=jnp.float32)
            off += n
        return out.astype(q.dtype)

    outs = lax.map(one_block, (qp, jnp.arange(n_blk)))
    return outs.transpose(1, 0, 2, 3, 4).reshape(b, n_blk * blk, h, dh)[:, :t]


def moe(x, w_router, b_router, w_gate_up, b_gate_up, w_down, b_down):
    bn, t, d = x.shape
    xt = x.reshape(-1, d)
    logits = jnp.matmul(xt, w_router).astype(jnp.float32) + b_router.astype(jnp.float32)
    top_v, top_i = lax.top_k(logits, TOP_K)
    top_w = jax.nn.softmax(top_v, axis=-1)
    gates = jnp.sum(jax.nn.one_hot(top_i, N_EXPERTS, dtype=jnp.float32) * top_w[..., None], axis=1)
    out = jnp.zeros((xt.shape[0], d), jnp.float32)
    for e in range(N_EXPERTS):
        hh = jnp.matmul(xt, w_gate_up[e]) + b_gate_up[e]
        g_ = jnp.minimum(hh[:, :D_FF], SWIGLU_LIMIT)
        u_ = jnp.clip(hh[:, D_FF:], -SWIGLU_LIMIT, SWIGLU_LIMIT)
        act = (u_ + 1.0) * g_ * jax.nn.sigmoid(SWIGLU_ALPHA * g_)
        out = out + gates[:, e:e + 1] * (jnp.matmul(act, w_down[e]) + b_down[e])
    return out.astype(x.dtype).reshape(bn, t, d)


def decoder_layer(h, p, conv_prev, past, q_pos0, g_mix, w_in, q_norm_w, k_norm_w, sb_bias, conv_w, w_out,
                  g_moe, w_router, b_router, w_gate_up, b_gate_up, w_down, b_down,
                  g_ple, w_ple_gate, w_ple_proj):
    bn, t, _ = h.shape
    a = rms_norm(h, g_mix)
    proj = jnp.matmul(a, w_in)
    q, k, v, gate_b, gate_c, u_in = jnp.split(proj, IN_SPLITS, axis=-1)
    q = rms_norm(q.reshape(bn, t, SB_HEADS, HEAD_DIM), q_norm_w)
    k = rms_norm(k.reshape(bn, t, SB_HEADS, HEAD_DIM), k_norm_w)
    v = v.reshape(bn, t, SB_HEADS, HEAD_DIM)
    new_seg = (k, v, q_pos0 + jnp.arange(t))
    if past is None:
        segments = [new_seg]
    else:
        k_past, v_past = past
        segments = [(k_past, v_past, jnp.arange(k_past.shape[1])), new_seg]
    attn = stick_breaking(q, segments, q_pos0, sb_bias).reshape(bn, t, ATTN_WIDTH)
    u = gate_c * u_in
    u_ext = jnp.concatenate([conv_prev.astype(u.dtype), u], axis=1)
    conv = conv_w[0] * u_ext[:, 0:t]
    for j in range(1, CONV_K):
        conv = conv + conv_w[j] * u_ext[:, j:j + t]
    conv_out = gate_b * conv
    new_conv = u_ext[:, -(CONV_K - 1):]
    h = h + jnp.matmul(jnp.concatenate([attn, conv_out], axis=-1), w_out)
    h = h + moe(rms_norm(h, g_moe), w_router, b_router, w_gate_up, b_gate_up, w_down, b_down)
    ple_gate = jax.nn.sigmoid(jnp.matmul(rms_norm(h, g_ple), w_ple_gate))
    h = h + ple_gate * jnp.matmul(p, w_ple_proj)
    return h, k, v, new_conv


def setup_inputs(seed: int = 0) -> dict:
    key = jax.random.key(seed)
    ks = jax.random.split(key, 25)
    f32 = jnp.float32
    n_pages = PAST_LEN // PAGE_SIZE
    n_used = DEC_BATCH * n_pages
    n_pool = (5 * n_used + 3) // 4

    def nrm(k, shape, scale):
        return jax.random.normal(k, shape, f32) * scale

    def gain(k, shape):
        return 1.0 + 0.05 * jax.random.normal(k, shape, f32)

    sb_bias = (-jnp.linspace(SB_BIAS_MIN, SB_BIAS_MAX, SB_HEADS, dtype=f32)[None, :]
               + nrm(ks[24], (DEPTH, SB_HEADS), 0.1))

    return {
        'x_prompt': nrm(ks[0], (BATCH, SEQ, D_MODEL), 1.0),
        'x_sample': nrm(ks[1], (DEC_BATCH, DEC_SEQ, D_MODEL), 1.0),
        'p_prompt': nrm(ks[2], (DEPTH, BATCH, SEQ, PLE_DIM), 1.0),
        'p_sample': nrm(ks[3], (DEPTH, DEC_BATCH, DEC_SEQ, PLE_DIM), 1.0),
        'cache_k': nrm(ks[4], (DEPTH, n_pool, PAGE_SIZE, SB_HEADS, HEAD_DIM), 1.0),
        'cache_v': nrm(ks[5], (DEPTH, n_pool, PAGE_SIZE, SB_HEADS, HEAD_DIM), 1.0),
        'state_conv': nrm(ks[6], (DEPTH, DEC_BATCH, CONV_K - 1, CONV_WIDTH), 1.0),
        'page_table': jax.random.permutation(ks[7], n_pool)[:n_used].reshape(DEC_BATCH, n_pages).astype(jnp.int32),
        'g_mix': gain(ks[8], (DEPTH, D_MODEL)),
        'w_in': nrm(ks[9], (DEPTH, D_MODEL, IN_WIDTH), D_MODEL ** -0.5),
        'q_norm_w': gain(ks[10], (DEPTH, HEAD_DIM)),
        'k_norm_w': gain(ks[11], (DEPTH, HEAD_DIM)),
        'sb_bias': sb_bias,
        'conv_w': nrm(ks[12], (DEPTH, CONV_K, CONV_WIDTH), CONV_K ** -0.5),
        'w_out': nrm(ks[13], (DEPTH, MIX_WIDTH, D_MODEL), MIX_WIDTH ** -0.5),
        'g_moe': gain(ks[14], (DEPTH, D_MODEL)),
        'w_router': nrm(ks[15], (DEPTH, D_MODEL, N_EXPERTS), D_MODEL ** -0.5),
        'b_router': nrm(ks[16], (DEPTH, N_EXPERTS), 0.01),
        'w_gate_up': nrm(ks[17], (DEPTH, N_EXPERTS, D_MODEL, 2 * D_FF), D_MODEL ** -0.5),
        'b_gate_up': nrm(ks[18], (DEPTH, N_EXPERTS, 2 * D_FF), 0.01),
        'w_down': nrm(ks[19], (DEPTH, N_EXPERTS, D_FF, D_MODEL), D_FF ** -0.5),
        'b_down': nrm(ks[20], (DEPTH, N_EXPERTS, D_MODEL), 0.01),
        'g_ple': gain(ks[21], (DEPTH, D_MODEL)),
        'w_ple_gate': nrm(ks[22], (DEPTH, D_MODEL, D_MODEL), D_MODEL ** -0.5),
        'w_ple_proj': nrm(ks[23], (DEPTH, PLE_DIM, D_MODEL), PLE_DIM ** -0.5),
    }


def reference(x_prompt, x_sample, p_prompt, p_sample, cache_k, cache_v, state_conv, page_table,
              g_mix, w_in, q_norm_w, k_norm_w, sb_bias, conv_w, w_out, g_moe, w_router, b_router,
              w_gate_up, b_gate_up, w_down, b_down, g_ple, w_ple_gate, w_ple_proj):
    h_p, h_s = x_prompt, x_sample
    dec_b = x_sample.shape[0]
    past_len = page_table.shape[1] * cache_k.shape[2]
    kp_l, vp_l, cp_l, ks_l, vs_l, cs_l = [], [], [], [], [], []
    for l in range(DEPTH):
        lw = (g_mix[l], w_in[l], q_norm_w[l], k_norm_w[l], sb_bias[l], conv_w[l], w_out[l], g_moe[l], w_router[l],
              b_router[l], w_gate_up[l], b_gate_up[l], w_down[l], b_down[l], g_ple[l], w_ple_gate[l], w_ple_proj[l])
        zero_conv = jnp.zeros((x_prompt.shape[0], CONV_K - 1, CONV_WIDTH), x_prompt.dtype)
        h_p, kp, vp, cp = decoder_layer(h_p, p_prompt[l], zero_conv, None, 0, *lw)
        k_past = jnp.take(cache_k[l], page_table, axis=0).reshape(dec_b, past_len, SB_HEADS, HEAD_DIM)
        v_past = jnp.take(cache_v[l], page_table, axis=0).reshape(dec_b, past_len, SB_HEADS, HEAD_DIM)
        h_s, k_s, v_s, c_s = decoder_layer(h_s, p_sample[l], state_conv[l], (k_past, v_past), past_len, *lw)
        kp_l.append(kp); vp_l.append(vp); cp_l.append(cp)
        ks_l.append(k_s); vs_l.append(v_s); cs_l.append(c_s)
    return (h_p, h_s, jnp.stack(kp_l), jnp.stack(vp_l), jnp.stack(cp_l), jnp.stack(ks_l), jnp.stack(vs_l), jnp.stack(cs_l))
```

```python
import functools
import math

import jax
import jax.numpy as jnp
from jax import lax
from jax.experimental import pallas as pl
from jax.experimental.pallas import tpu as pltpu

F32 = jnp.float32
BF16 = jnp.bfloat16
I32 = jnp.int32

D_MODEL = 1024
N_HEADS = 8
HEAD_DIM = 64
ATTN_W = N_HEADS * HEAD_DIM
CONV_W = 512
N_EXPERTS = 32
TOP_K = 4
D_FF = 1024
RMS_EPS = 1e-6
SWIGLU_LIMIT = 7.0
SWIGLU_ALPHA = 1.702
PAGE = 128

LANES = 128
HALO = 8
VMEM_LIMIT = 56 * 1024 * 1024

TM_IN = 512
BK = 256
PAGES_PER_STEP = 8
TM_MIX = 256
TM_MOE = 256
TM_PLE = 256


def _cparams(sem):
    return pltpu.CompilerParams(dimension_semantics=sem, vmem_limit_bytes=VMEM_LIMIT)


def _rms(x, g):
    ms = jnp.mean(x * x, axis=-1, keepdims=True)
    return x * lax.rsqrt(ms + RMS_EPS) * g


def _log_keep(z):
    return jnp.minimum(-z, 0.0) - jnp.log(1.0 + jnp.exp(-jnp.abs(z)))


def _inproj_core(x, g, w_ref, bd, qnw, knw):
    a = _rms(x, g).astype(BF16)

    def sec(i):
        return jnp.dot(a, w_ref[:, i * 512:(i + 1) * 512], preferred_element_type=F32)

    def head_norm(t, w):
        m = jnp.dot((t * t).astype(BF16), bd, preferred_element_type=F32)
        return t * lax.rsqrt(m + RMS_EPS) * w

    qn = head_norm(sec(0), qnw)
    kn = head_norm(sec(1), knw)
    v = sec(2)
    gate_b = sec(3)
    u = sec(4) * sec(5)
    return qn, kn, v, gate_b, u


def _inproj_prompt_kernel(x_ref, g_ref, w_ref, bd_ref, qnw_ref, knw_ref, cw_ref,
                          qw_ref, k_ref, kb_ref, v_ref, vt_ref, co_ref, ut_ref, ubuf):
    tm = x_ref.shape[0]

    @pl.when(pl.program_id(1) == 0)
    def _():
        ubuf[0:HALO, :] = jnp.zeros((HALO, CONV_W), F32)

    qn, kn, v, gate_b, u = _inproj_core(x_ref[...], g_ref[...], w_ref, bd_ref[...], qnw_ref[...], knw_ref[...])

    lane = lax.broadcasted_iota(I32, (1, LANES), 1)
    lo = (lane < HEAD_DIM).astype(F32)
    hi = 1.0 - lo
    for h in range(N_HEADS):
        p = h // 2
        blk = qn[:, p * LANES:(p + 1) * LANES] * (lo if h % 2 == 0 else hi)
        qw_ref[:, h * LANES:(h + 1) * LANES] = blk.astype(BF16)

    k_ref[...] = kn
    kb_ref[...] = kn.astype(BF16)
    v_ref[...] = v
    vt = v.T
    for c in range(tm // BK):
        vt_ref[c] = vt[:, c * BK:(c + 1) * BK].astype(BF16)

    ubuf[HALO:HALO + tm, :] = u
    conv = (cw_ref[0:1, :] * ubuf[HALO - 2:HALO - 2 + tm, :]
            + cw_ref[1:2, :] * ubuf[HALO - 1:HALO - 1 + tm, :]
            + cw_ref[2:3, :] * u)
    co_ref[...] = (gate_b * conv).astype(BF16)
    tail = u[tm - HALO:tm, :]
    ubuf[0:HALO, :] = tail
    ut_ref[...] = tail


def _inproj_prompt(x, g, w_in, bd, qnw, knw, cw, batch, seq):
    rows = batch * seq
    nblk = seq // TM_IN
    row_map = lambda b, i: (b * nblk + i, 0)
    const = lambda b, i: (0, 0)
    out_shape = (
        jax.ShapeDtypeStruct((rows, N_HEADS * LANES), BF16),
        jax.ShapeDtypeStruct((rows, ATTN_W), F32),
        jax.ShapeDtypeStruct((rows, ATTN_W), BF16),
        jax.ShapeDtypeStruct((rows, ATTN_W), F32),
        jax.ShapeDtypeStruct((batch, seq // BK, ATTN_W, BK), BF16),
        jax.ShapeDtypeStruct((rows, CONV_W), BF16),
        jax.ShapeDtypeStruct((batch, HALO, CONV_W), F32),
    )
    return pl.pallas_call(
        _inproj_prompt_kernel,
        grid=(batch, nblk),
        in_specs=[
            pl.BlockSpec((TM_IN, D_MODEL), row_map),
            pl.BlockSpec((1, D_MODEL), const),
            pl.BlockSpec(w_in.shape, const),
            pl.BlockSpec(bd.shape, const),
            pl.BlockSpec((1, ATTN_W), const),
            pl.BlockSpec((1, ATTN_W), const),
            pl.BlockSpec(cw.shape, const),
        ],
        out_specs=(
            pl.BlockSpec((TM_IN, N_HEADS * LANES), row_map),
            pl.BlockSpec((TM_IN, ATTN_W), row_map),
            pl.BlockSpec((TM_IN, ATTN_W), row_map),
            pl.BlockSpec((TM_IN, ATTN_W), row_map),
            pl.BlockSpec((None, TM_IN // BK, ATTN_W, BK), lambda b, i: (b, i, 0, 0)),
            pl.BlockSpec((TM_IN, CONV_W), row_map),
            pl.BlockSpec((None, HALO, CONV_W), lambda b, i: (b, 0, 0)),
        ),
        out_shape=out_shape,
        scratch_shapes=[pltpu.VMEM((HALO + TM_IN, CONV_W), F32)],
        compiler_params=_cparams(("arbitrary", "arbitrary")),
        name="inproj_prompt",
    )(x, g, w_in, bd, qnw, knw, cw)


def _inproj_sample_kernel(x_ref, g_ref, w_ref, bd_ref, qnw_ref, knw_ref, cw_ref, h1_ref, h2_ref,
                          q_ref, k_ref, v_ref, co_ref, u_ref, ubuf, *, seq):
    rows = x_ref.shape[0]
    qn, kn, v, gate_b, u = _inproj_core(x_ref[...], g_ref[...], w_ref, bd_ref[...], qnw_ref[...], knw_ref[...])
    q_ref[...] = qn
    k_ref[...] = kn
    v_ref[...] = v
    u_ref[...] = u
    ubuf[0:HALO, :] = jnp.zeros((HALO, CONV_W), F32)
    ubuf[HALO:HALO + rows, :] = u
    t = lax.broadcasted_iota(I32, (rows, 1), 0) % seq
    x1 = jnp.where(t >= 1, ubuf[HALO - 1:HALO - 1 + rows, :], h1_ref[...])
    x2 = jnp.where(t >= 2, ubuf[HALO - 2:HALO - 2 + rows, :], h2_ref[...])
    conv = cw_ref[0:1, :] * x2 + cw_ref[1:2, :] * x1 + cw_ref[2:3, :] * u
    co_ref[...] = (gate_b * conv).astype(BF16)


def _inproj_sample(x, g, w_in, bd, qnw, knw, cw, halo1, halo2, seq):
    rows = x.shape[0]
    full = lambda a: pl.BlockSpec(a.shape, lambda i: (0,) * a.ndim)
    args = (x, g, w_in, bd, qnw, knw, cw, halo1, halo2)
    shp = lambda w, dt: jax.ShapeDtypeStruct((rows, w), dt)
    out_shape = (shp(ATTN_W, F32), shp(ATTN_W, F32), shp(ATTN_W, F32), shp(CONV_W, BF16), shp(CONV_W, F32))
    return pl.pallas_call(
        functools.partial(_inproj_sample_kernel, seq=seq),
        grid=(1,),
        in_specs=[full(a) for a in args],
        out_specs=tuple(pl.BlockSpec(s.shape, lambda i: (0, 0)) for s in out_shape),
        out_shape=out_shape,
        scratch_shapes=[pltpu.VMEM((HALO + rows, CONV_W), F32)],
        compiler_params=_cparams(("arbitrary",)),
        name="inproj_sample",
    )(*args)


def _attn_prompt_kernel(bias_ref, q_ref, kp_ref, vt_ref, ut_ref, o_ref):
    p = pl.program_id(1)
    i = pl.program_id(2)
    tq = q_ref.shape[0]
    key = lax.broadcasted_iota(I32, (BK, tq), 0)
    qry = lax.broadcasted_iota(I32, (BK, tq), 1)
    causal = key < qry
    ut = ut_ref[...]
    outs = []
    for hh in range(2):
        bias = bias_ref[2 * p + hh]
        qh = q_ref[:, hh * LANES:(hh + 1) * LANES]

        def tile(j, carry, acc, masked, hh=hh, bias=bias, qh=qh):
            start = pl.multiple_of(j * BK, BK)
            kblk = kp_ref[pl.ds(start, BK), :]
            z = lax.dot_general(kblk, qh, (((1,), (1,)), ((), ())), preferred_element_type=F32) + bias
            lk = _log_keep(z)
            if masked:
                lk = jnp.where(causal, lk, 0.0)
            later = jnp.dot(ut, lk.astype(BF16), preferred_element_type=F32)
            w = jnp.exp(z + lk + later + carry)
            if masked:
                w = jnp.where(causal, w, 0.0)
            vblk = vt_ref[j, hh * HEAD_DIM:(hh + 1) * HEAD_DIM, :]
            acc = acc + jnp.dot(vblk, w.astype(BF16), preferred_element_type=F32)
            carry = carry + later[0:1, :] + lk[0:1, :]
            return carry, acc

        carry, acc = tile(i, jnp.zeros((1, tq), F32), jnp.zeros((HEAD_DIM, tq), F32), True)
        carry, acc = lax.fori_loop(0, i, lambda it, c: tile(i - 1 - it, c[0], c[1], False), (carry, acc))
        outs.append(acc)
    o_ref[...] = jnp.concatenate(outs, axis=0).T.astype(BF16)


def _attn_prompt(sb_bias, q_wide, k_bf, vt, ut, batch, seq):
    nq = seq // BK
    return pl.pallas_call(
        _attn_prompt_kernel,
        grid_spec=pltpu.PrefetchScalarGridSpec(
            num_scalar_prefetch=0,
            grid=(batch, N_HEADS // 2, nq),
            in_specs=[
                pl.BlockSpec(memory_space=pltpu.SMEM),
                pl.BlockSpec((BK, 2 * LANES), lambda b, p, i: (b * nq + i, p)),
                pl.BlockSpec((seq, LANES), lambda b, p, i: (b, p)),
                pl.BlockSpec((None, nq, LANES, BK), lambda b, p, i: (b, 0, p, 0)),
                pl.BlockSpec((BK, BK), lambda b, p, i: (0, 0)),
            ],
            out_specs=pl.BlockSpec((BK, LANES), lambda b, p, i: (b * nq + i, p)),
        ),
        out_shape=jax.ShapeDtypeStruct((batch * seq, ATTN_W), BF16),
        compiler_params=_cparams(("arbitrary", "arbitrary", "arbitrary")),
        name="attn_prompt",
    )(sb_bias, q_wide, k_bf, vt, ut)


def _attn_sample_kernel(pt_ref, bias_ref, q_ref, kn_ref, vn_ref, *rest, seq):
    del pt_ref
    npg = PAGES_PER_STEP
    k_refs = rest[:npg]
    v_refs = rest[npg:2 * npg]
    ucat_ref, bm_ref, o_ref, qbd, acc, carry = rest[2 * npg:]
    j = pl.program_id(1)
    rows = N_HEADS * seq

    def page(kpg, vpg, masked):
        z = lax.dot_general(qbd[...], kpg, (((1,), (1,)), ((), ())), preferred_element_type=F32) + bias_ref[...]
        lk = _log_keep(z)
        if masked:
            t = lax.broadcasted_iota(I32, (rows, PAGE), 0) % seq
            s = lax.broadcasted_iota(I32, (rows, PAGE), 1)
            m = s < t
            lk = jnp.where(m, lk, 0.0)
        lt = jnp.dot(lk.astype(BF16), ucat_ref[...], preferred_element_type=F32)
        w = jnp.exp(z + lk + lt[:, :PAGE] + carry[...])
        if masked:
            w = jnp.where(m, w, 0.0)
        acc[...] += jnp.dot(w.astype(BF16), vpg, preferred_element_type=F32)
        carry[...] += lt[:, PAGE:]

    @pl.when(j == 0)
    def _():
        q8 = jnp.concatenate([q_ref[...]] * N_HEADS, axis=0)
        qbd[...] = (q8 * bm_ref[...]).astype(BF16)
        acc[...] = jnp.zeros_like(acc)
        carry[...] = jnp.zeros_like(carry)
        page(kn_ref[...], vn_ref[...], True)

    for i in range(npg):
        page(k_refs[i][...].astype(BF16), v_refs[i][...].astype(BF16), False)

    @pl.when(j == pl.num_programs(1) - 1)
    def _():
        d = acc[...] * bm_ref[...]
        out = d[0:seq, :]
        for h in range(1, N_HEADS):
            out = out + d[h * seq:(h + 1) * seq, :]
        o_ref[...] = out


def _attn_sample(page_table, bias_rows, q_s, kn_pad, vn_pad, cache_k, cache_v, ucat, bm, seq):
    dec_b, n_pages = page_table.shape
    npg = PAGES_PER_STEP
    steps = n_pages // npg

    def page_spec(i):
        return pl.BlockSpec((None, PAGE, ATTN_W),
                            lambda b, j, pt, i=i: (pt[b, n_pages - 1 - (j * npg + i)], 0, 0))

    const2 = lambda b, j, pt: (0, 0)
    per_seq = lambda b, j, pt: (b, 0, 0)
    in_specs = [
        pl.BlockSpec(bias_rows.shape, const2),
        pl.BlockSpec((None, seq, ATTN_W), per_seq),
        pl.BlockSpec((None, PAGE, ATTN_W), per_seq),
        pl.BlockSpec((None, PAGE, ATTN_W), per_seq),
    ] + [page_spec(i) for i in range(npg)] + [page_spec(i) for i in range(npg)] + [
        pl.BlockSpec(ucat.shape, const2),
        pl.BlockSpec(bm.shape, const2),
    ]
    return pl.pallas_call(
        functools.partial(_attn_sample_kernel, seq=seq),
        grid_spec=pltpu.PrefetchScalarGridSpec(
            num_scalar_prefetch=1,
            grid=(dec_b, steps),
            in_specs=in_specs,
            out_specs=pl.BlockSpec((None, seq, ATTN_W), per_seq),
            scratch_shapes=[pltpu.VMEM((N_HEADS * seq, ATTN_W), BF16),
                            pltpu.VMEM((N_HEADS * seq, ATTN_W), F32),
                            pltpu.VMEM((N_HEADS * seq, PAGE), F32)],
        ),
        out_shape=jax.ShapeDtypeStruct((dec_b, seq, ATTN_W), F32),
        compiler_params=_cparams(("arbitrary", "arbitrary")),
        name="attn_sample",
    )(page_table, bias_rows, q_s, kn_pad, vn_pad, *([cache_k] * npg), *([cache_v] * npg), ucat, bm)


def _mixout_kernel(hp_ref, atp_ref, cop_ref, hs_ref, ats_ref, cos_ref, woa_ref, woc_ref, g_ref, wr1_ref, wr2_ref,
                   br_ref, ltri_ref, h1_ref, xn_ref, ti_ref, tw_ref, rk_ref, cnt_ref, cnt_sc, *, np_blk):
    tm = hp_ref.shape[0]
    i = pl.program_id(0)

    @pl.when(i == 0)
    def _():
        cnt_sc[...] = jnp.zeros_like(cnt_sc)

    is_s = i >= np_blk
    h = jnp.where(is_s, hs_ref[...], hp_ref[...])
    at = jnp.where(is_s, ats_ref[...], atp_ref[...])
    co = jnp.where(is_s, cos_ref[...], cop_ref[...])
    h1 = (h + jnp.dot(at, woa_ref[...], preferred_element_type=F32)
          + jnp.dot(co, woc_ref[...], preferred_element_type=F32))
    h1_ref[...] = h1
    xn = _rms(h1, g_ref[...])
    xn_ref[...] = xn

    xh = xn.astype(BF16)
    xl = (xn - xh.astype(F32)).astype(BF16)
    t1 = jnp.dot(xh, wr1_ref[...], preferred_element_type=F32)
    t2 = jnp.dot(xl, wr2_ref[...], preferred_element_type=F32)
    logits = t1[:, :LANES] + t1[:, LANES:] + t2 + br_ref[...]

    lane_i = lax.broadcasted_iota(I32, (tm, LANES), 1)
    lane = lane_i.astype(F32)
    cur = logits
    vals, idxs, hots = [], [], []
    for _ in range(TOP_K):
        m = jnp.max(cur, axis=-1, keepdims=True)
        idx = jnp.min(jnp.where(cur == m, lane, float(LANES)), axis=-1, keepdims=True)
        hot = lane == idx
        cur = jnp.where(hot, -jnp.inf, cur)
        vals.append(m)
        idxs.append(idx)
        hots.append(hot)
    es = [jnp.exp(v - vals[0]) for v in vals]
    denom = es[0] + es[1] + es[2] + es[3]

    member = hots[0] | hots[1] | hots[2] | hots[3]
    mf = member.astype(F32)
    before = jnp.dot(ltri_ref[...], mf.astype(BF16), preferred_element_type=F32) + cnt_sc[...]
    cnt_sc[...] += jnp.sum(mf, axis=0, keepdims=True)
    cnt_ref[...] = cnt_sc[...]

    ti = jnp.zeros((tm, LANES), I32)
    tw = jnp.zeros((tm, LANES), F32)
    rk = jnp.zeros((tm, LANES), I32)
    for c in range(TOP_K):
        sel = lane_i == c
        rank = jnp.sum(jnp.where(hots[c], before, 0.0), axis=-1, keepdims=True)
        ti = jnp.where(sel, idxs[c].astype(I32), ti)
        tw = jnp.where(sel, es[c] / denom, tw)
        rk = jnp.where(sel, rank.astype(I32), rk)
    ti_ref[...] = ti
    tw_ref[...] = tw
    rk_ref[...] = rk


def _mixout(hp, atp, cop, hs, ats, cos, woa, woc, g, wr1, wr2, br):
    tm = TM_MIX
    np_blk = hp.shape[0] // tm
    ns_blk = hs.shape[0] // tm
    n_all = hp.shape[0] + hs.shape[0]
    ltri = jnp.tril(jnp.ones((tm, tm), F32), -1).astype(BF16)
    const = lambda i: (0, 0)
    row_p = lambda i: (jnp.minimum(i, np_blk - 1), 0)
    row_s = lambda i: (jnp.maximum(i - np_blk, 0), 0)
    row_out = lambda i: (i, 0)
    ins = [hp, atp, cop, hs, ats, cos, woa, woc, g, wr1, wr2, br, ltri]
    in_specs = [
        pl.BlockSpec((tm, D_MODEL), row_p),
        pl.BlockSpec((tm, ATTN_W), row_p),
        pl.BlockSpec((tm, CONV_W), row_p),
        pl.BlockSpec((tm, D_MODEL), row_s),
        pl.BlockSpec((tm, ATTN_W), row_s),
        pl.BlockSpec((tm, CONV_W), row_s),
        pl.BlockSpec(woa.shape, const),
        pl.BlockSpec(woc.shape, const),
        pl.BlockSpec(g.shape, const),
        pl.BlockSpec(wr1.shape, const),
        pl.BlockSpec(wr2.shape, const),
        pl.BlockSpec(br.shape, const),
        pl.BlockSpec(ltri.shape, const),
    ]
    out_shape = (
        jax.ShapeDtypeStruct((n_all, D_MODEL), F32),
        jax.ShapeDtypeStruct((n_all, D_MODEL), F32),
        jax.ShapeDtypeStruct((n_all, LANES), I32),
        jax.ShapeDtypeStruct((n_all, LANES), F32),
        jax.ShapeDtypeStruct((n_all, LANES), I32),
    )
    outs = pl.pallas_call(
        functools.partial(_mixout_kernel, np_blk=np_blk),
        grid=(np_blk + ns_blk,),
        in_specs=in_specs,
        out_specs=(
            pl.BlockSpec((tm, D_MODEL), row_out),
            pl.BlockSpec((tm, D_MODEL), row_out),
            pl.BlockSpec((tm, LANES), row_out),
            pl.BlockSpec((tm, LANES), row_out),
            pl.BlockSpec((tm, LANES), row_out),
            pl.BlockSpec((1, LANES), const),
        ),
        out_shape=out_shape + (jax.ShapeDtypeStruct((1, LANES), F32),),
        scratch_shapes=[pltpu.VMEM((1, LANES), F32)],
        compiler_params=_cparams(("arbitrary",)),
        name="mixout",
    )(*ins)
    return outs[:5], outs[5]


def _moe_kernel(te_ref, nu_ref, src_cur, src_nxt, xn_hbm, wgu_ref, bgu_ref, wd_ref, bd_ref,
                y_ref, xbuf, sem, wgu_bf, wd_bf):
    t = pl.program_id(0)
    n_used = nu_ref[0]
    slot = t % 2
    tm = y_ref.shape[0]

    def row_copy(tok, r, s):
        return pltpu.make_async_copy(xn_hbm.at[pl.ds(tok, 1)], xbuf.at[s, pl.ds(r, 1)], sem.at[s])

    def gather(src, s):
        def body(r, c):
            row_copy(src[0, 0, r], r, s).start()
            return c
        lax.fori_loop(0, tm, body, 0, unroll=8)

    def drain(s):
        def body(r, c):
            row_copy(0, r, s).wait()
            return c
        lax.fori_loop(0, tm, body, 0, unroll=8)

    @pl.when(t == 0)
    def _():
        gather(src_cur, 0)

    @pl.when(t + 1 < n_used)
    def _():
        gather(src_nxt, 1 - slot)

    @pl.when(t >= n_used)
    def _():
        y_ref[...] = jnp.zeros_like(y_ref)

    @pl.when(t < n_used)
    def _():
        prev_e = te_ref[jnp.maximum(t - 1, 0)]

        @pl.when((t == 0) | (te_ref[t] != prev_e))
        def _():
            wgu_bf[...] = wgu_ref[...].astype(BF16)
            wd_bf[...] = wd_ref[...].astype(BF16)

        drain(slot)
        x = xbuf[slot].astype(BF16)
        hh = jnp.dot(x, wgu_bf[...], preferred_element_type=F32) + bgu_ref[...]
        g = jnp.minimum(hh[:, :D_FF], SWIGLU_LIMIT)
        u = jnp.clip(hh[:, D_FF:], -SWIGLU_LIMIT, SWIGLU_LIMIT)
        act = (u + 1.0) * g * jax.nn.sigmoid(SWIGLU_ALPHA * g)
        y_ref[...] = jnp.dot(act.astype(BF16), wd_bf[...], preferred_element_type=F32) + bd_ref[...]


def _moe(tile_expert, n_used, src, xn_all, wgu, bgu, wd, bd):
    n_tiles = src.shape[0]
    tm = src.shape[2]
    last = n_tiles - 1
    e_map = lambda t, te, nu: (te[t], 0, 0)
    return pl.pallas_call(
        _moe_kernel,
        grid_spec=pltpu.PrefetchScalarGridSpec(
            num_scalar_prefetch=2,
            grid=(n_tiles,),
            in_specs=[
                pl.BlockSpec((1, 1, tm), lambda t, te, nu: (t, 0, 0), memory_space=pltpu.SMEM),
                pl.BlockSpec((1, 1, tm), lambda t, te, nu: (jnp.minimum(t + 1, last), 0, 0),
                             memory_space=pltpu.SMEM),
                pl.BlockSpec(memory_space=pl.ANY),
                pl.BlockSpec((None, D_MODEL, 2 * D_FF), e_map),
                pl.BlockSpec((None, 1, 2 * D_FF), e_map),
                pl.BlockSpec((None, D_FF, D_MODEL), e_map),
                pl.BlockSpec((None, 1, D_MODEL), e_map),
            ],
            out_specs=pl.BlockSpec((tm, D_MODEL), lambda t, te, nu: (t, 0)),
            scratch_shapes=[pltpu.VMEM((2, tm, D_MODEL), F32),
                            pltpu.SemaphoreType.DMA((2,)),
                            pltpu.VMEM((D_MODEL, 2 * D_FF), BF16),
                            pltpu.VMEM((D_FF, D_MODEL), BF16)],
        ),
        out_shape=jax.ShapeDtypeStruct((n_tiles * tm, D_MODEL), F32),
        compiler_params=_cparams(("arbitrary",)),
        name="moe",
    )(tile_expert, n_used, src, src, xn_all, wgu, bgu, wd, bd)


def _ple_kernel(dst_cur, dst_nxt, y_hbm, h1_ref, tw_ref, p_ref, g_ref, wg_ref, wp_ref, o_ref, ybuf, sem):
    t = pl.program_id(0)
    nt = pl.num_programs(0)
    slot = t % 2
    tm = o_ref.shape[0]

    def row_copy(row, c, r, s):
        return pltpu.make_async_copy(y_hbm.at[pl.ds(row, 1)], ybuf.at[s, c, pl.ds(r, 1)], sem.at[s])

    def gather(dst, s):
        def body(r, carry):
            for c in range(TOP_K):
                row_copy(dst[0, 0, r * TOP_K + c], c, r, s).start()
            return carry
        lax.fori_loop(0, tm, body, 0, unroll=4)

    def drain(s):
        def body(r, carry):
            for c in range(TOP_K):
                row_copy(0, c, r, s).wait()
            return carry
        lax.fori_loop(0, tm, body, 0, unroll=4)

    @pl.when(t == 0)
    def _():
        gather(dst_cur, 0)

    @pl.when(t + 1 < nt)
    def _():
        gather(dst_nxt, 1 - slot)

    drain(slot)
    tw = tw_ref[...]
    moe = tw[:, 0:1] * ybuf[slot, 0]
    for c in range(1, TOP_K):
        moe = moe + tw[:, c:c + 1] * ybuf[slot, c]
    h2 = h1_ref[...] + moe
    gate = jax.nn.sigmoid(jnp.dot(_rms(h2, g_ref[...]).astype(BF16), wg_ref[...], preferred_element_type=F32))
    proj = jnp.dot(p_ref[...].astype(BF16), wp_ref[...], preferred_element_type=F32)
    o_ref[...] = h2 + gate * proj


def _ple(dest, y_sorted, h1_all, tw_all, p, g, wg, wp, row_off):
    rows = p.shape[0]
    tm = min(TM_PLE, rows)
    nblk = rows // tm
    blk_off = row_off // tm
    last = nblk - 1
    const = lambda t: (0, 0)
    return pl.pallas_call(
        _ple_kernel,
        grid=(nblk,),
        in_specs=[
            pl.BlockSpec((1, 1, tm * TOP_K), lambda t: (t + blk_off, 0, 0), memory_space=pltpu.SMEM),
            pl.BlockSpec((1, 1, tm * TOP_K), lambda t: (jnp.minimum(t + 1, last) + blk_off, 0, 0),
                         memory_space=pltpu.SMEM),
            pl.BlockSpec(memory_space=pl.ANY),
            pl.BlockSpec((tm, D_MODEL), lambda t: (t + blk_off, 0)),
            pl.BlockSpec((tm, LANES), lambda t: (t + blk_off, 0)),
            pl.BlockSpec((tm, p.shape[1]), lambda t: (t, 0)),
            pl.BlockSpec(g.shape, const),
            pl.BlockSpec(wg.shape, const),
            pl.BlockSpec(wp.shape, const),
        ],
        out_specs=pl.BlockSpec((tm, D_MODEL), lambda t: (t, 0)),
        out_shape=jax.ShapeDtypeStruct((rows, D_MODEL), F32),
        scratch_shapes=[pltpu.VMEM((2, TOP_K, tm, D_MODEL), F32), pltpu.SemaphoreType.DMA((2,))],
        compiler_params=_cparams(("arbitrary",)),
        name="ple_%d" % rows,
    )(dest, dest, y_sorted, h1_all, tw_all, p, g, wg, wp)


def kernel(x_prompt, x_sample, p_prompt, p_sample, cache_k, cache_v, state_conv, page_table, g_mix, w_in,
           q_norm_w, k_norm_w, sb_bias, conv_w, w_out, g_moe, w_router, b_router, w_gate_up, b_gate_up, w_down,
           b_down, g_ple, w_ple_gate, w_ple_proj):
    depth = g_mix.shape[0]
    assert depth == 1, "single-layer step"
    batch, seq, _ = x_prompt.shape
    dec_b, dec_seq, _ = x_sample.shape
    n_pool = cache_k.shape[1]
    n_p = batch * seq
    n_s = dec_b * dec_seq
    n_all = n_p + n_s
    assert seq % TM_IN == 0 and TM_IN % BK == 0 and n_p % TM_MIX == 0 and n_s % TM_MIX == 0
    assert n_p % TM_PLE == 0 and n_s % TM_PLE == 0
    assert page_table.shape[1] % PAGES_PER_STEP == 0 and cache_k.shape[2] == PAGE

    scale = 1.0 / math.sqrt(HEAD_DIM)
    w_in_bf = w_in[0].astype(BF16)
    head_id = jnp.arange(ATTN_W) // HEAD_DIM
    bd = jnp.where(head_id[:, None] == head_id[None, :], 1.0 / HEAD_DIM, 0.0).astype(BF16)
    qnw = (jnp.tile(q_norm_w[0], N_HEADS) * scale)[None, :]
    knw = jnp.tile(k_norm_w[0], N_HEADS)[None, :]
    g_mix2 = g_mix[0][None, :]
    cw = conv_w[0]
    woa = w_out[0, :ATTN_W].astype(BF16)
    woc = w_out[0, ATTN_W:].astype(BF16)
    wr = jnp.pad(w_router[0], ((0, 0), (0, LANES - N_EXPERTS)))
    wr_hi = wr.astype(BF16)
    wr_lo = (wr - wr_hi.astype(F32)).astype(BF16)
    wr1 = jnp.concatenate([wr_hi, wr_lo], axis=1)
    br = jnp.concatenate([b_router[0], jnp.full((LANES - N_EXPERTS,), -1e30, F32)])[None, :]
    g_moe2 = g_moe[0][None, :]
    g_ple2 = g_ple[0][None, :]
    wpg = w_ple_gate[0].astype(BF16)
    wpp = w_ple_proj[0].astype(BF16)

    xp = x_prompt.reshape(n_p, D_MODEL)
    xs = x_sample.reshape(n_s, D_MODEL)
    qw_p, k_p, kb_p, v_p, vt_p, co_p, ut_p = _inproj_prompt(xp, g_mix2, w_in_bf, bd, qnw, knw, cw, batch, seq)
    st = state_conv[0]
    zero_s = jnp.zeros((dec_b, dec_seq, CONV_W), F32)
    halo1 = zero_s.at[:, 0].set(st[:, 1]).reshape(n_s, CONV_W)
    halo2 = zero_s.at[:, 0].set(st[:, 0]).at[:, 1].set(st[:, 1]).reshape(n_s, CONV_W)
    q_s, k_s, v_s, co_s, u_s = _inproj_sample(xs, g_mix2, w_in_bf, bd, qnw, knw, cw, halo1, halo2, dec_seq)

    ut = jnp.triu(jnp.ones((BK, BK), F32), 1).astype(BF16)
    attn_p = _attn_prompt(sb_bias[0], qw_p, kb_p, vt_p, ut, batch, seq)

    pad_rows = ((0, 0), (0, PAGE - dec_seq), (0, 0))
    kn_pad = jnp.pad(k_s.reshape(dec_b, dec_seq, ATTN_W).astype(BF16), pad_rows)
    vn_pad = jnp.pad(v_s.reshape(dec_b, dec_seq, ATTN_W).astype(BF16), pad_rows)
    row_head = jnp.arange(N_HEADS * dec_seq) // dec_seq
    bias_rows = jnp.broadcast_to(sb_bias[0][row_head][:, None], (N_HEADS * dec_seq, PAGE)).astype(F32)
    bm = (row_head[:, None] == head_id[None, :]).astype(F32)
    ucat = jnp.concatenate([jnp.tril(jnp.ones((PAGE, PAGE), F32), -1), jnp.ones((PAGE, PAGE), F32)],
                           axis=1).astype(BF16)
    attn_s = _attn_sample(page_table, bias_rows, q_s.reshape(dec_b, dec_seq, ATTN_W), kn_pad, vn_pad,
                          cache_k.reshape(n_pool, PAGE, ATTN_W), cache_v.reshape(n_pool, PAGE, ATTN_W),
                          ucat, bm, dec_seq)

    bufs, cnt = _mixout(xp, attn_p, co_p, xs, attn_s.reshape(n_s, ATTN_W).astype(BF16), co_s, woa, woc, g_moe2,
                        wr1, wr_hi, br)
    h1_all, xn_all, ti_all, tw_all, rk_all = bufs

    n_tiles = (n_all * TOP_K + N_EXPERTS * (TM_MOE - 1)) // TM_MOE + 1
    counts = cnt[0, :N_EXPERTS].astype(I32)
    padded = ((counts + TM_MOE - 1) // TM_MOE) * TM_MOE
    ends = jnp.cumsum(padded)
    starts = ends - padded
    top_i = ti_all[:, :TOP_K]
    dest = starts[top_i] + rk_all[:, :TOP_K]
    tile_expert = jnp.minimum(jnp.searchsorted(ends, jnp.arange(n_tiles, dtype=I32) * TM_MOE, side="right"),
                              N_EXPERTS - 1).astype(I32)
    n_used = (ends[-1] // TM_MOE).astype(I32)[None]
    token = jnp.arange(n_all * TOP_K, dtype=I32) // TOP_K
    src = jnp.zeros((n_tiles * TM_MOE,), I32).at[dest.reshape(-1)].set(token)
    src = src.reshape(n_tiles, 1, TM_MOE)

    y_sorted = _moe(tile_expert, n_used, src, xn_all, w_gate_up[0], b_gate_up[0][:, None, :], w_down[0],
                    b_down[0][:, None, :])

    dest_blk = dest.reshape(n_all // TM_PLE, 1, TM_PLE * TOP_K)
    y_p = _ple(dest_blk, y_sorted, h1_all, tw_all, p_prompt[0].reshape(n_p, -1), g_ple2, wpg, wpp, 0)
    y_s = _ple(dest_blk, y_sorted, h1_all, tw_all, p_sample[0].reshape(n_s, -1), g_ple2, wpg, wpp, n_p)

    new_conv_p = ut_p[:, HALO - 2:, :]
    new_conv_s = u_s.reshape(dec_b, dec_seq, CONV_W)[:, dec_seq - 2:, :]
    return (
        y_p.reshape(batch, seq, D_MODEL),
        y_s.reshape(dec_b, dec_seq, D_MODEL),
        k_p.reshape(1, batch, seq, N_HEADS, HEAD_DIM),
        v_p.reshape(1, batch, seq, N_HEADS, HEAD_DIM),
        new_conv_p[None],
        k_s.reshape(1, dec_b, dec_seq, N_HEADS, HEAD_DIM),
        v_s.reshape(1, dec_b, dec_seq, N_HEADS, HEAD_DIM),
        new_conv_s[None],
    )
```

```python
import functools
import math

import jax
import jax.numpy as jnp
from jax import lax
from jax.experimental import pallas as pl
from jax.experimental.pallas import tpu as pltpu

F32 = jnp.float32
BF16 = jnp.bfloat16
I32 = jnp.int32

D_MODEL = 1024
N_HEADS = 8
HEAD_DIM = 64
ATTN_W = N_HEADS * HEAD_DIM
CONV_W = 512
N_EXPERTS = 32
TOP_K = 4
D_FF = 1024
RMS_EPS = 1e-6
SWIGLU_LIMIT = 7.0
SWIGLU_ALPHA = 1.702
PAGE = 128

LANES = 128
HALO = 8
VMEM_LIMIT = 56 * 1024 * 1024

TM_IN = 512
BK = 256
TQ_ATT = 512
PAGES_PER_STEP = 16
TM_MIX = 256
TM_MOE = 512
MOE_CHUNK = 256
TM_PLE = 256


def _cparams(sem):
    return pltpu.CompilerParams(dimension_semantics=sem, vmem_limit_bytes=VMEM_LIMIT)


ROW_SUB = D_MODEL // LANES


def _store_rows(ref, x):
    rows = x.shape[0]
    for c in range(ROW_SUB):
        ref[pl.ds(c, rows, stride=ROW_SUB), :] = x[:, c * LANES:(c + 1) * LANES]


def _load_rows(ref):
    rows = ref.shape[0] // ROW_SUB
    return jnp.concatenate([ref[pl.ds(c, rows, stride=ROW_SUB), :] for c in range(ROW_SUB)], axis=1)


def _rms(x, g):
    ms = jnp.mean(x * x, axis=-1, keepdims=True)
    return x * lax.rsqrt(ms + RMS_EPS) * g


def _softplus2(z2):
    return jnp.maximum(z2, 0.0) + jnp.log2(1.0 + jnp.exp2(jnp.minimum(z2, -z2)))


def _inproj_core(x, g, w_ref, bd, qnw, knw):
    a = _rms(x, g).astype(BF16)

    def sec(i):
        return jnp.dot(a, w_ref[:, i * 512:(i + 1) * 512], preferred_element_type=F32)

    def head_norm(t, w):
        m = jnp.dot((t * t).astype(BF16), bd, preferred_element_type=F32)
        return t * lax.rsqrt(m + RMS_EPS) * w

    qn = head_norm(sec(0), qnw)
    kn = head_norm(sec(1), knw)
    v = sec(2)
    gate_b = sec(3)
    u = sec(4) * sec(5)
    return qn, kn, v, gate_b, u


def _inproj_prompt_kernel(x_ref, g_ref, w_ref, bd_ref, qnw_ref, knw_ref, cw_ref,
                          qw_ref, kt_ref, vt_ref, ktb_ref, vtb_ref, co_ref, ut_ref, ubuf):
    tm = x_ref.shape[0]

    @pl.when(pl.program_id(1) == 0)
    def _():
        ubuf[0:HALO, :] = jnp.zeros((HALO, CONV_W), F32)

    qn, kn, v, gate_b, u = _inproj_core(x_ref[...], g_ref[...], w_ref, bd_ref[...], qnw_ref[...], knw_ref[...])

    lane = lax.broadcasted_iota(I32, (1, LANES), 1)
    lo = (lane < HEAD_DIM).astype(F32)
    hi = 1.0 - lo
    for h in range(N_HEADS):
        p = h // 2
        blk = qn[:, p * LANES:(p + 1) * LANES] * (lo if h % 2 == 0 else hi)
        qw_ref[:, h * LANES:(h + 1) * LANES] = blk.astype(BF16)

    kt = kn.T
    vt = v.T
    kt_ref[...] = kt
    vt_ref[...] = vt
    for c in range(tm // BK):
        ktb_ref[c] = kt[:, c * BK:(c + 1) * BK].astype(BF16)
        vtb_ref[c] = vt[:, c * BK:(c + 1) * BK].astype(BF16)

    ubuf[HALO:HALO + tm, :] = u
    conv = (cw_ref[0:1, :] * ubuf[HALO - 2:HALO - 2 + tm, :]
            + cw_ref[1:2, :] * ubuf[HALO - 1:HALO - 1 + tm, :]
            + cw_ref[2:3, :] * u)
    co_ref[...] = (gate_b * conv).astype(BF16)
    tail = u[tm - HALO:tm, :]
    ubuf[0:HALO, :] = tail
    ut_ref[...] = tail


def _inproj_prompt(x, g, w_in, bd, qnw, knw, cw, batch, seq):
    rows = batch * seq
    nblk = seq // TM_IN
    row_map = lambda b, i: (b * nblk + i, 0)
    const = lambda b, i: (0, 0)
    out_shape = (
        jax.ShapeDtypeStruct((rows, N_HEADS * LANES), BF16),
        jax.ShapeDtypeStruct((batch, ATTN_W, seq), F32),
        jax.ShapeDtypeStruct((batch, ATTN_W, seq), F32),
        jax.ShapeDtypeStruct((batch, seq // BK, ATTN_W, BK), BF16),
        jax.ShapeDtypeStruct((batch, seq // BK, ATTN_W, BK), BF16),
        jax.ShapeDtypeStruct((rows, CONV_W), BF16),
        jax.ShapeDtypeStruct((batch, HALO, CONV_W), F32),
    )
    return pl.pallas_call(
        _inproj_prompt_kernel,
        grid=(batch, nblk),
        in_specs=[
            pl.BlockSpec((TM_IN, D_MODEL), row_map),
            pl.BlockSpec((1, D_MODEL), const),
            pl.BlockSpec(w_in.shape, const),
            pl.BlockSpec(bd.shape, const),
            pl.BlockSpec((1, ATTN_W), const),
            pl.BlockSpec((1, ATTN_W), const),
            pl.BlockSpec(cw.shape, const),
        ],
        out_specs=(
            pl.BlockSpec((TM_IN, N_HEADS * LANES), row_map),
            pl.BlockSpec((None, ATTN_W, TM_IN), lambda b, i: (b, 0, i)),
            pl.BlockSpec((None, ATTN_W, TM_IN), lambda b, i: (b, 0, i)),
            pl.BlockSpec((None, TM_IN // BK, ATTN_W, BK), lambda b, i: (b, i, 0, 0)),
            pl.BlockSpec((None, TM_IN // BK, ATTN_W, BK), lambda b, i: (b, i, 0, 0)),
            pl.BlockSpec((TM_IN, CONV_W), row_map),
            pl.BlockSpec((None, HALO, CONV_W), lambda b, i: (b, 0, 0)),
        ),
        out_shape=out_shape,
        scratch_shapes=[pltpu.VMEM((HALO + TM_IN, CONV_W), F32)],
        compiler_params=_cparams(("arbitrary", "arbitrary")),
        name="inproj_prompt",
    )(x, g, w_in, bd, qnw, knw, cw)


def _inproj_sample_kernel(x_ref, g_ref, w_ref, bd_ref, qnw_ref, knw_ref, cw_ref, h1_ref, h2_ref,
                          q_ref, k_ref, v_ref, co_ref, u_ref, ubuf, *, seq):
    rows = x_ref.shape[0]
    qn, kn, v, gate_b, u = _inproj_core(x_ref[...], g_ref[...], w_ref, bd_ref[...], qnw_ref[...], knw_ref[...])
    q_ref[...] = qn
    k_ref[...] = kn
    v_ref[...] = v
    u_ref[...] = u
    ubuf[0:HALO, :] = jnp.zeros((HALO, CONV_W), F32)
    ubuf[HALO:HALO + rows, :] = u
    t = lax.broadcasted_iota(I32, (rows, 1), 0) % seq
    x1 = jnp.where(t >= 1, ubuf[HALO - 1:HALO - 1 + rows, :], h1_ref[...])
    x2 = jnp.where(t >= 2, ubuf[HALO - 2:HALO - 2 + rows, :], h2_ref[...])
    conv = cw_ref[0:1, :] * x2 + cw_ref[1:2, :] * x1 + cw_ref[2:3, :] * u
    co_ref[...] = (gate_b * conv).astype(BF16)


def _inproj_sample(x, g, w_in, bd, qnw, knw, cw, halo1, halo2, seq):
    rows = x.shape[0]
    full = lambda a: pl.BlockSpec(a.shape, lambda i: (0,) * a.ndim)
    args = (x, g, w_in, bd, qnw, knw, cw, halo1, halo2)
    shp = lambda w, dt: jax.ShapeDtypeStruct((rows, w), dt)
    out_shape = (shp(ATTN_W, F32), shp(ATTN_W, F32), shp(ATTN_W, F32), shp(CONV_W, BF16), shp(CONV_W, F32))
    return pl.pallas_call(
        functools.partial(_inproj_sample_kernel, seq=seq),
        grid=(1,),
        in_specs=[full(a) for a in args],
        out_specs=tuple(pl.BlockSpec(s.shape, lambda i: (0, 0)) for s in out_shape),
        out_shape=out_shape,
        scratch_shapes=[pltpu.VMEM((HALO + rows, CONV_W), F32)],
        compiler_params=_cparams(("arbitrary",)),
        name="inproj_sample",
    )(*args)


def _attn_prompt_kernel(bias_ref, q_ref, kt_ref, vt_ref, u_ref, o_ref):
    p = pl.program_id(1)
    i = pl.program_id(2)
    tq = q_ref.shape[0]
    kq = tq // BK
    row = lax.broadcasted_iota(I32, (2 * tq, 1), 0)
    second = row >= tq
    bias = jnp.where(second, bias_ref[2 * p + 1], bias_ref[2 * p])
    qry = jnp.where(second, row - tq, row)
    key = lax.broadcasted_iota(I32, (2 * tq, BK), 1)
    feat = lax.broadcasted_iota(I32, (LANES, 2 * BK), 0)
    half = lax.broadcasted_iota(I32, (LANES, 2 * BK), 1) >= BK
    own = (feat >= HEAD_DIM) == half
    u = u_ref[...]
    lane = lax.broadcasted_iota(I32, (1, LANES), 1)
    b_hi = bias.astype(BF16).astype(F32)
    b_cols = jnp.where(lane == 0, b_hi, jnp.where(lane == 1, bias - b_hi, 0.0)).astype(BF16)
    q2 = jnp.concatenate([q_ref[:, 0:LANES], q_ref[:, LANES:2 * LANES]], axis=0)
    q2 = jnp.concatenate([q2, b_cols], axis=1)
    ones_rows = (lax.broadcasted_iota(I32, (LANES, BK), 0) < 2).astype(BF16)

    def tile(j, carry, acc, causal=None):
        kt = jnp.concatenate([kt_ref[j], ones_rows], axis=0)
        vt = vt_ref[j]
        z = jnp.dot(q2, kt, preferred_element_type=F32)
        sp = _softplus2(z)
        if causal is not None:
            sp = jnp.where(causal, sp, 0.0)
        later = jnp.dot(sp.astype(BF16), u, preferred_element_type=F32)
        w = jnp.exp2(z - sp - later + carry)
        if causal is not None:
            w = jnp.where(causal, w, 0.0)
        wb = w.astype(BF16)
        w2 = jnp.concatenate([wb[:tq], wb[tq:]], axis=1)
        v2 = jnp.concatenate([vt, vt], axis=1)
        v2 = jnp.where(own, v2, jnp.zeros_like(v2))
        acc = acc + lax.dot_general(w2, v2, (((1,), (1,)), ((), ())), preferred_element_type=F32)
        carry = carry - jnp.sum(sp, axis=1, keepdims=True)
        return carry, acc

    state = (jnp.zeros((2 * tq, 1), F32), jnp.zeros((tq, LANES), F32))
    for d in reversed(range(kq)):
        state = tile(kq * i + d, state[0], state[1], key + d * BK < qry)

    def pair(it, c):
        j = kq * i - 1 - 2 * it
        c = tile(j, c[0], c[1])
        return tile(j - 1, c[0], c[1])

    carry, acc = lax.fori_loop(0, (kq * i) // 2, pair, state)
    o_ref[...] = acc.astype(BF16)


def _attn_prompt(bias2, q_wide, ktb, vtb, u, batch, seq):
    assert (TQ_ATT // BK) % 2 == 0, "unmasked key tiles are consumed in pairs"
    nq = seq // TQ_ATT
    kv_spec = pl.BlockSpec((None, seq // BK, LANES, BK), lambda b, p, i: (b, 0, p, 0))
    return pl.pallas_call(
        _attn_prompt_kernel,
        grid=(batch, N_HEADS // 2, nq),
        in_specs=[
            pl.BlockSpec(memory_space=pltpu.SMEM),
            pl.BlockSpec((TQ_ATT, 2 * LANES), lambda b, p, i: (b * nq + i, p)),
            kv_spec,
            kv_spec,
            pl.BlockSpec((BK, BK), lambda b, p, i: (0, 0)),
        ],
        out_specs=pl.BlockSpec((TQ_ATT, LANES), lambda b, p, i: (b * nq + i, p)),
        out_shape=jax.ShapeDtypeStruct((batch * seq, ATTN_W), BF16),
        compiler_params=_cparams(("arbitrary", "arbitrary", "arbitrary")),
        name="attn_prompt",
    )(bias2, q_wide, ktb, vtb, u)


def _attn_sample_kernel(pt_ref, bias_ref, q_ref, kn_ref, vn_ref, *rest, seq):
    del pt_ref
    npg = PAGES_PER_STEP
    k_refs = rest[:npg]
    v_refs = rest[npg:2 * npg]
    ucat_ref, bm_ref, o_ref, qbd, acc, carry = rest[2 * npg:]
    j = pl.program_id(1)
    rows = N_HEADS * seq

    def page(ktp, vtp, masked):
        z = jnp.dot(qbd[...], ktp, preferred_element_type=F32) + bias_ref[...]
        sp = _softplus2(z)
        if masked:
            t = lax.broadcasted_iota(I32, (rows, PAGE), 0) % seq
            s = lax.broadcasted_iota(I32, (rows, PAGE), 1)
            m = s < t
            sp = jnp.where(m, sp, 0.0)
        lt = jnp.dot(sp.astype(BF16), ucat_ref[...], preferred_element_type=F32)
        w = jnp.exp2(z - sp - lt[:, :PAGE] + carry[...])
        if masked:
            w = jnp.where(m, w, 0.0)
        acc[...] += lax.dot_general(w.astype(BF16), vtp, (((1,), (1,)), ((), ())), preferred_element_type=F32)
        carry[...] -= lt[:, PAGE:]

    @pl.when(j == 0)
    def _():
        q8 = jnp.concatenate([q_ref[...]] * N_HEADS, axis=0)
        qbd[...] = (q8 * bm_ref[...]).astype(BF16)
        acc[...] = jnp.zeros_like(acc)
        carry[...] = jnp.zeros_like(carry)
        page(kn_ref[...], vn_ref[...], True)

    ktc = jnp.concatenate([r[...].reshape(ATTN_W, PAGE).astype(BF16) for r in k_refs], axis=1)
    vtc = jnp.concatenate([r[...].reshape(ATTN_W, PAGE).astype(BF16) for r in v_refs], axis=1)
    zc = jnp.dot(qbd[...], ktc, preferred_element_type=F32)
    run = carry[...]
    ws = []
    for i in range(npg):
        z = zc[:, i * PAGE:(i + 1) * PAGE] + bias_ref[...]
        sp = _softplus2(z)
        lt = jnp.dot(sp.astype(BF16), ucat_ref[...], preferred_element_type=F32)
        ws.append(jnp.exp2(z - sp - lt[:, :PAGE] + run).astype(BF16))
        run = run - lt[:, PAGE:]
    carry[...] = run
    acc[...] += lax.dot_general(jnp.concatenate(ws, axis=1), vtc, (((1,), (1,)), ((), ())),
                                preferred_element_type=F32)

    @pl.when(j == pl.num_programs(1) - 1)
    def _():
        d = acc[...] * bm_ref[...]
        out = d[0:seq, :]
        for h in range(1, N_HEADS):
            out = out + d[h * seq:(h + 1) * seq, :]
        o_ref[...] = out


def _attn_sample(page_table, bias_rows, q_s, kn_pad, vn_pad, cache_k, cache_v, ucat, bm, seq):
    dec_b, n_pages = page_table.shape
    npg = PAGES_PER_STEP
    steps = n_pages // npg

    def page_spec(i):
        return pl.BlockSpec((None, N_HEADS, HEAD_DIM, PAGE),
                            lambda b, j, pt, i=i: (pt[b, n_pages - 1 - (j * npg + i)], 0, 0, 0))

    const2 = lambda b, j, pt: (0, 0)
    per_seq = lambda b, j, pt: (b, 0, 0)
    in_specs = [
        pl.BlockSpec(bias_rows.shape, const2),
        pl.BlockSpec((None, seq, ATTN_W), per_seq),
        pl.BlockSpec((None, ATTN_W, PAGE), per_seq),
        pl.BlockSpec((None, ATTN_W, PAGE), per_seq),
    ] + [page_spec(i) for i in range(npg)] + [page_spec(i) for i in range(npg)] + [
        pl.BlockSpec(ucat.shape, const2),
        pl.BlockSpec(bm.shape, const2),
    ]
    return pl.pallas_call(
        functools.partial(_attn_sample_kernel, seq=seq),
        grid_spec=pltpu.PrefetchScalarGridSpec(
            num_scalar_prefetch=1,
            grid=(dec_b, steps),
            in_specs=in_specs,
            out_specs=pl.BlockSpec((None, seq, ATTN_W), per_seq),
            scratch_shapes=[pltpu.VMEM((N_HEADS * seq, ATTN_W), BF16),
                            pltpu.VMEM((N_HEADS * seq, ATTN_W), F32),
                            pltpu.VMEM((N_HEADS * seq, PAGE), F32)],
        ),
        out_shape=jax.ShapeDtypeStruct((dec_b, seq, ATTN_W), F32),
        compiler_params=_cparams(("arbitrary", "arbitrary")),
        name="attn_sample",
    )(page_table, bias_rows, q_s, kn_pad, vn_pad, *([cache_k] * npg), *([cache_v] * npg), ucat, bm)


def _mixout_kernel(hp_ref, atp_ref, cop_ref, hs_ref, ats_ref, cos_ref, woa_ref, woc_ref, g_ref, wr1_ref, wr2_ref,
                   br_ref, ltri_ref, h1_ref, xn_ref, ti_ref, tw_ref, rk_ref, cnt_ref, cnt_sc, *, np_blk):
    tm = hp_ref.shape[0]
    i = pl.program_id(0)

    @pl.when(i == 0)
    def _():
        cnt_sc[...] = jnp.zeros_like(cnt_sc)

    is_s = i >= np_blk
    h = jnp.where(is_s, hs_ref[...], hp_ref[...])
    at = jnp.where(is_s, ats_ref[...], atp_ref[...])
    co = jnp.where(is_s, cos_ref[...], cop_ref[...])
    h1 = (h + jnp.dot(at, woa_ref[...], preferred_element_type=F32)
          + jnp.dot(co, woc_ref[...], preferred_element_type=F32))
    h1_ref[...] = h1
    xn = _rms(h1, g_ref[...])
    _store_rows(xn_ref, xn)

    xh = xn.astype(BF16)
    xl = (xn - xh.astype(F32)).astype(BF16)
    t1 = jnp.dot(xh, wr1_ref[...], preferred_element_type=F32)
    t2 = jnp.dot(xl, wr2_ref[...], preferred_element_type=F32)
    logits = t1[:, :LANES] + t1[:, LANES:] + t2 + br_ref[...]

    lane_i = lax.broadcasted_iota(I32, (tm, LANES), 1)
    lane = lane_i.astype(F32)
    cur = logits
    vals, idxs, hots = [], [], []
    for _ in range(TOP_K):
        m = jnp.max(cur, axis=-1, keepdims=True)
        idx = jnp.min(jnp.where(cur == m, lane, float(LANES)), axis=-1, keepdims=True)
        hot = lane == idx
        cur = jnp.where(hot, -jnp.inf, cur)
        vals.append(m)
        idxs.append(idx)
        hots.append(hot)
    es = [jnp.exp(v - vals[0]) for v in vals]
    denom = es[0] + es[1] + es[2] + es[3]

    member = hots[0] | hots[1] | hots[2] | hots[3]
    mf = member.astype(F32)
    before = jnp.dot(ltri_ref[...], mf.astype(BF16), preferred_element_type=F32) + cnt_sc[...]
    cnt_sc[...] += jnp.sum(mf, axis=0, keepdims=True)
    cnt_ref[...] = cnt_sc[...]

    ti = jnp.zeros((tm, LANES), I32)
    tw = jnp.zeros((tm, LANES), F32)
    rk = jnp.zeros((tm, LANES), I32)
    for c in range(TOP_K):
        sel = lane_i == c
        rank = jnp.sum(jnp.where(hots[c], before, 0.0), axis=-1, keepdims=True)
        ti = jnp.where(sel, idxs[c].astype(I32), ti)
        tw = jnp.where(sel, es[c] / denom, tw)
        rk = jnp.where(sel, rank.astype(I32), rk)
    ti_ref[...] = ti
    tw_ref[...] = tw
    rk_ref[...] = rk


def _mixout(hp, atp, cop, hs, ats, cos, woa, woc, g, wr1, wr2, br):
    tm = TM_MIX
    np_blk = hp.shape[0] // tm
    ns_blk = hs.shape[0] // tm
    n_all = hp.shape[0] + hs.shape[0]
    ltri = jnp.tril(jnp.ones((tm, tm), F32), -1).astype(BF16)
    const = lambda i: (0, 0)
    row_p = lambda i: (jnp.minimum(i, np_blk - 1), 0)
    row_s = lambda i: (jnp.maximum(i - np_blk, 0), 0)
    row_out = lambda i: (i, 0)
    ins = [hp, atp, cop, hs, ats, cos, woa, woc, g, wr1, wr2, br, ltri]
    in_specs = [
        pl.BlockSpec((tm, D_MODEL), row_p),
        pl.BlockSpec((tm, ATTN_W), row_p),
        pl.BlockSpec((tm, CONV_W), row_p),
        pl.BlockSpec((tm, D_MODEL), row_s),
        pl.BlockSpec((tm, ATTN_W), row_s),
        pl.BlockSpec((tm, CONV_W), row_s),
        pl.BlockSpec(woa.shape, const),
        pl.BlockSpec(woc.shape, const),
        pl.BlockSpec(g.shape, const),
        pl.BlockSpec(wr1.shape, const),
        pl.BlockSpec(wr2.shape, const),
        pl.BlockSpec(br.shape, const),
        pl.BlockSpec(ltri.shape, const),
    ]
    out_shape = (
        jax.ShapeDtypeStruct((n_all, D_MODEL), F32),
        jax.ShapeDtypeStruct((n_all * ROW_SUB, LANES), F32),
        jax.ShapeDtypeStruct((n_all, LANES), I32),
        jax.ShapeDtypeStruct((n_all, LANES), F32),
        jax.ShapeDtypeStruct((n_all, LANES), I32),
    )
    outs = pl.pallas_call(
        functools.partial(_mixout_kernel, np_blk=np_blk),
        grid=(np_blk + ns_blk,),
        in_specs=in_specs,
        out_specs=(
            pl.BlockSpec((tm, D_MODEL), row_out),
            pl.BlockSpec((tm * ROW_SUB, LANES), row_out),
            pl.BlockSpec((tm, LANES), row_out),
            pl.BlockSpec((tm, LANES), row_out),
            pl.BlockSpec((tm, LANES), row_out),
            pl.BlockSpec((1, LANES), const),
        ),
        out_shape=out_shape + (jax.ShapeDtypeStruct((1, LANES), F32),),
        scratch_shapes=[pltpu.VMEM((1, LANES), F32)],
        compiler_params=_cparams(("arbitrary",)),
        name="mixout",
    )(*ins)
    return outs[:5], outs[5]


def _moe_kernel(te_ref, nu_ref, src_cur, src_nxt, xn_hbm, wgu_ref, bgu_ref, wd_ref, bd_ref,
                y_ref, xbuf, sem, wgu_bf, wd_bf):
    t = pl.program_id(0)
    n_used = nu_ref[0]
    slot = t % 2
    tm = y_ref.shape[0] // ROW_SUB

    def row_copy(tok, r, s):
        return pltpu.make_async_copy(xn_hbm.at[pl.ds(pl.multiple_of(tok * ROW_SUB, ROW_SUB), ROW_SUB)],
                                     xbuf.at[s, pl.ds(pl.multiple_of(r * ROW_SUB, ROW_SUB), ROW_SUB)], sem.at[s])

    def gather(src, s):
        def body(r, c):
            row_copy(src[0, 0, r], r, s).start()
            return c
        lax.fori_loop(0, tm, body, 0, unroll=16)

    def drain(s):
        def body(r, c):
            row_copy(0, r, s).wait()
            return c
        lax.fori_loop(0, tm, body, 0, unroll=16)

    @pl.when(t == 0)
    def _():
        gather(src_cur, 0)

    @pl.when(t >= n_used)
    def _():
        gather(src_nxt, 1 - slot)
        drain(slot)
        y_ref[...] = jnp.zeros_like(y_ref)

    @pl.when(t < n_used)
    def _():
        prev_e = te_ref[jnp.maximum(t - 1, 0)]

        @pl.when((t == 0) | (te_ref[t] != prev_e))
        def _():
            wgu_bf[...] = wgu_ref[...].astype(BF16)
            wd_bf[...] = wd_ref[...].astype(BF16)

        drain(slot)
        x = _load_rows(xbuf.at[slot]).astype(BF16)
        n_chunk = (2 * D_FF) // MOE_CHUNK
        per = tm // n_chunk
        hs = []
        for n in range(n_chunk):
            for r in range(n * per, (n + 1) * per):
                row_copy(src_nxt[0, 0, r], r, 1 - slot).start()
            cols = slice(n * MOE_CHUNK, (n + 1) * MOE_CHUNK)
            hs.append(jnp.dot(x, wgu_bf[:, cols], preferred_element_type=F32) + bgu_ref[:, cols])
        hh = jnp.concatenate(hs, axis=1)
        g = jnp.minimum(hh[:, :D_FF], SWIGLU_LIMIT)
        u = jnp.clip(hh[:, D_FF:], -SWIGLU_LIMIT, SWIGLU_LIMIT)
        act = (u + 1.0) * g * jax.nn.sigmoid(SWIGLU_ALPHA * g)
        _store_rows(y_ref, jnp.dot(act.astype(BF16), wd_bf[...], preferred_element_type=F32) + bd_ref[...])

    @pl.when(t == pl.num_programs(0) - 1)
    def _():
        drain(1 - slot)


def _moe(tile_expert, n_used, src, xn_all, wgu, bgu, wd, bd):
    n_tiles = src.shape[0]
    tm = src.shape[2]
    last = n_tiles - 1
    e_map = lambda t, te, nu: (te[t], 0, 0)
    return pl.pallas_call(
        _moe_kernel,
        grid_spec=pltpu.PrefetchScalarGridSpec(
            num_scalar_prefetch=2,
            grid=(n_tiles,),
            in_specs=[
                pl.BlockSpec((1, 1, tm), lambda t, te, nu: (t, 0, 0), memory_space=pltpu.SMEM),
                pl.BlockSpec((1, 1, tm), lambda t, te, nu: (jnp.minimum(t + 1, last), 0, 0),
                             memory_space=pltpu.SMEM),
                pl.BlockSpec(memory_space=pl.ANY),
                pl.BlockSpec((None, D_MODEL, 2 * D_FF), e_map),
                pl.BlockSpec((None, 1, 2 * D_FF), e_map),
                pl.BlockSpec((None, D_FF, D_MODEL), e_map),
                pl.BlockSpec((None, 1, D_MODEL), e_map),
            ],
            out_specs=pl.BlockSpec((tm * ROW_SUB, LANES), lambda t, te, nu: (t, 0)),
            scratch_shapes=[pltpu.VMEM((2, tm * ROW_SUB, LANES), F32),
                            pltpu.SemaphoreType.DMA((2,)),
                            pltpu.VMEM((D_MODEL, 2 * D_FF), BF16),
                            pltpu.VMEM((D_FF, D_MODEL), BF16)],
        ),
        out_shape=jax.ShapeDtypeStruct((n_tiles * tm * ROW_SUB, LANES), F32),
        compiler_params=_cparams(("arbitrary",)),
        name="moe",
    )(tile_expert, n_used, src, src, xn_all, wgu, bgu, wd, bd)


def _ple_kernel(dst_cur, dst_nxt, y_hbm, h1_ref, tw_ref, p_ref, g_ref, wg_ref, wp_ref, o_ref, ybuf, sem):
    t = pl.program_id(0)
    nt = pl.num_programs(0)
    slot = t % 2
    tm = o_ref.shape[0]

    def row_copy(row, c, r, s):
        return pltpu.make_async_copy(y_hbm.at[pl.ds(pl.multiple_of(row * ROW_SUB, ROW_SUB), ROW_SUB)],
                                     ybuf.at[s, c, pl.ds(pl.multiple_of(r * ROW_SUB, ROW_SUB), ROW_SUB)], sem.at[s])

    def gather(dst, s):
        def body(r, carry):
            for c in range(TOP_K):
                row_copy(dst[0, 0, r * TOP_K + c], c, r, s).start(priority=c % 2)
            return carry
        lax.fori_loop(0, tm, body, 0, unroll=4)

    def drain(s):
        def body(r, carry):
            for c in range(TOP_K):
                row_copy(0, c, r, s).wait()
            return carry
        lax.fori_loop(0, tm, body, 0, unroll=4)

    @pl.when(t == 0)
    def _():
        gather(dst_cur, 0)

    @pl.when(t + 1 < nt)
    def _():
        gather(dst_nxt, 1 - slot)

    drain(slot)
    tw = tw_ref[...]
    moe = tw[:, 0:1] * _load_rows(ybuf.at[slot, 0])
    for c in range(1, TOP_K):
        moe = moe + tw[:, c:c + 1] * _load_rows(ybuf.at[slot, c])
    h2 = h1_ref[...] + moe
    gate = jax.nn.sigmoid(jnp.dot(_rms(h2, g_ref[...]).astype(BF16), wg_ref[...], preferred_element_type=F32))
    proj = jnp.dot(p_ref[...].astype(BF16), wp_ref[...], preferred_element_type=F32)
    o_ref[...] = h2 + gate * proj


def _ple(dest, y_sorted, h1_all, tw_all, p, g, wg, wp, row_off):
    rows = p.shape[0]
    tm = min(TM_PLE, rows)
    nblk = rows // tm
    blk_off = row_off // tm
    last = nblk - 1
    const = lambda t: (0, 0)
    return pl.pallas_call(
        _ple_kernel,
        grid=(nblk,),
        in_specs=[
            pl.BlockSpec((1, 1, tm * TOP_K), lambda t: (t + blk_off, 0, 0), memory_space=pltpu.SMEM),
            pl.BlockSpec((1, 1, tm * TOP_K), lambda t: (jnp.minimum(t + 1, last) + blk_off, 0, 0),
                         memory_space=pltpu.SMEM),
            pl.BlockSpec(memory_space=pl.ANY),
            pl.BlockSpec((tm, D_MODEL), lambda t: (t + blk_off, 0)),
            pl.BlockSpec((tm, LANES), lambda t: (t + blk_off, 0)),
            pl.BlockSpec((tm, p.shape[1]), lambda t: (t, 0)),
            pl.BlockSpec(g.shape, const),
            pl.BlockSpec(wg.shape, const),
            pl.BlockSpec(wp.shape, const),
        ],
        out_specs=pl.BlockSpec((tm, D_MODEL), lambda t: (t, 0)),
        out_shape=jax.ShapeDtypeStruct((rows, D_MODEL), F32),
        scratch_shapes=[pltpu.VMEM((2, TOP_K, tm * ROW_SUB, LANES), F32), pltpu.SemaphoreType.DMA((2,))],
        compiler_params=_cparams(("arbitrary",)),
        name="ple_%d" % rows,
    )(dest, dest, y_sorted, h1_all, tw_all, p, g, wg, wp)


def kernel(x_prompt, x_sample, p_prompt, p_sample, cache_k, cache_v, state_conv, page_table, g_mix, w_in,
           q_norm_w, k_norm_w, sb_bias, conv_w, w_out, g_moe, w_router, b_router, w_gate_up, b_gate_up, w_down,
           b_down, g_ple, w_ple_gate, w_ple_proj):
    depth = g_mix.shape[0]
    assert depth == 1, "single-layer step"
    batch, seq, _ = x_prompt.shape
    dec_b, dec_seq, _ = x_sample.shape
    n_pool = cache_k.shape[1]
    n_p = batch * seq
    n_s = dec_b * dec_seq
    n_all = n_p + n_s
    assert seq % TM_IN == 0 and TM_IN % BK == 0 and n_p % TM_MIX == 0 and n_s % TM_MIX == 0
    assert seq % TQ_ATT == 0 and TQ_ATT % BK == 0
    assert n_p % TM_PLE == 0 and n_s % TM_PLE == 0
    assert page_table.shape[1] % PAGES_PER_STEP == 0 and cache_k.shape[2] == PAGE

    log2e = math.log2(math.e)
    scale2 = log2e / math.sqrt(HEAD_DIM)
    bias2 = sb_bias[0] * log2e
    w_in_bf = w_in[0].astype(BF16)
    head_id = jnp.arange(ATTN_W) // HEAD_DIM
    bd = jnp.where(head_id[:, None] == head_id[None, :], 1.0 / HEAD_DIM, 0.0).astype(BF16)
    qnw = (jnp.tile(q_norm_w[0], N_HEADS) * scale2)[None, :]
    knw = jnp.tile(k_norm_w[0], N_HEADS)[None, :]
    g_mix2 = g_mix[0][None, :]
    cw = conv_w[0]
    woa = w_out[0, :ATTN_W].astype(BF16)
    woc = w_out[0, ATTN_W:].astype(BF16)
    wr = jnp.pad(w_router[0], ((0, 0), (0, LANES - N_EXPERTS)))
    wr_hi = wr.astype(BF16)
    wr_lo = (wr - wr_hi.astype(F32)).astype(BF16)
    wr1 = jnp.concatenate([wr_hi, wr_lo], axis=1)
    br = jnp.concatenate([b_router[0], jnp.full((LANES - N_EXPERTS,), -1e30, F32)])[None, :]
    g_moe2 = g_moe[0][None, :]
    g_ple2 = g_ple[0][None, :]
    wpg = w_ple_gate[0].astype(BF16)
    wpp = w_ple_proj[0].astype(BF16)

    xp = x_prompt.reshape(n_p, D_MODEL)
    xs = x_sample.reshape(n_s, D_MODEL)
    qw_p, kt_p, vt_p, ktb_p, vtb_p, co_p, ut_p = _inproj_prompt(xp, g_mix2, w_in_bf, bd, qnw, knw, cw, batch, seq)
    st = state_conv[0]
    zero_s = jnp.zeros((dec_b, dec_seq, CONV_W), F32)
    halo1 = zero_s.at[:, 0].set(st[:, 1]).reshape(n_s, CONV_W)
    halo2 = zero_s.at[:, 0].set(st[:, 0]).at[:, 1].set(st[:, 1]).reshape(n_s, CONV_W)
    q_s, k_s, v_s, co_s, u_s = _inproj_sample(xs, g_mix2, w_in_bf, bd, qnw, knw, cw, halo1, halo2, dec_seq)

    u_tri = jnp.tril(jnp.ones((BK, BK), F32), -1).astype(BF16)
    attn_p = _attn_prompt(bias2, qw_p, ktb_p, vtb_p, u_tri, batch, seq)

    def new_page(a):
        at = jnp.swapaxes(a.reshape(dec_b, dec_seq, ATTN_W), 1, 2).astype(BF16)
        return jnp.pad(at, ((0, 0), (0, 0), (0, PAGE - dec_seq)))

    row_head = jnp.arange(N_HEADS * dec_seq) // dec_seq
    bias_rows = jnp.broadcast_to(bias2[row_head][:, None], (N_HEADS * dec_seq, PAGE)).astype(F32)
    bm = (row_head[:, None] == head_id[None, :]).astype(F32)
    ucat = jnp.concatenate([jnp.tril(jnp.ones((PAGE, PAGE), F32), -1), jnp.ones((PAGE, PAGE), F32)],
                           axis=1).astype(BF16)
    ck = jnp.transpose(cache_k[0], (0, 2, 3, 1))
    cv = jnp.transpose(cache_v[0], (0, 2, 3, 1))
    attn_s = _attn_sample(page_table, bias_rows, q_s.reshape(dec_b, dec_seq, ATTN_W), new_page(k_s), new_page(v_s),
                          ck, cv, ucat, bm, dec_seq)

    bufs, cnt = _mixout(xp, attn_p, co_p, xs, attn_s.reshape(n_s, ATTN_W).astype(BF16), co_s, woa, woc, g_moe2,
                        wr1, wr_hi, br)
    h1_all, xn_all, ti_all, tw_all, rk_all = bufs

    n_tiles = (n_all * TOP_K + N_EXPERTS * (TM_MOE - 1)) // TM_MOE + 1
    counts = cnt[0, :N_EXPERTS].astype(I32)
    padded = ((counts + TM_MOE - 1) // TM_MOE) * TM_MOE
    ends = jnp.cumsum(padded)
    starts = ends - padded
    top_i = ti_all[:, :TOP_K]
    dest = starts[top_i] + rk_all[:, :TOP_K]
    tile_start = jnp.arange(n_tiles, dtype=I32) * TM_MOE
    tile_expert = jnp.minimum(jnp.sum((tile_start[:, None] >= ends[None, :]).astype(I32), axis=1), N_EXPERTS - 1)
    n_used = (ends[-1] // TM_MOE).astype(I32)[None]
    token = jnp.arange(n_all * TOP_K, dtype=I32) // TOP_K
    src = jnp.zeros((n_tiles * TM_MOE,), I32).at[dest.reshape(-1)].set(token)
    src = src.reshape(n_tiles, 1, TM_MOE)

    y_sorted = _moe(tile_expert, n_used, src, xn_all, w_gate_up[0], b_gate_up[0][:, None, :], w_down[0],
                    b_down[0][:, None, :])

    dest_blk = dest.reshape(n_all // TM_PLE, 1, TM_PLE * TOP_K)
    y_p = _ple(dest_blk, y_sorted, h1_all, tw_all, p_prompt[0].reshape(n_p, -1), g_ple2, wpg, wpp, 0)
    y_s = _ple(dest_blk, y_sorted, h1_all, tw_all, p_sample[0].reshape(n_s, -1), g_ple2, wpg, wpp, n_p)

    new_conv_p = ut_p[:, HALO - 2:, :]
    new_conv_s = u_s.reshape(dec_b, dec_seq, CONV_W)[:, dec_seq - 2:, :]
    return (
        y_p.reshape(batch, seq, D_MODEL),
        y_s.reshape(dec_b, dec_seq, D_MODEL),
        jnp.transpose(kt_p.reshape(batch, N_HEADS, HEAD_DIM, seq), (0, 3, 1, 2))[None],
        jnp.transpose(vt_p.reshape(batch, N_HEADS, HEAD_DIM, seq), (0, 3, 1, 2))[None],
        new_conv_p[None],
        k_s.reshape(1, dec_b, dec_seq, N_HEADS, HEAD_DIM),
        v_s.reshape(1, dec_b, dec_seq, N_HEADS, HEAD_DIM),
        new_conv_s[None],
    )
```

```python
import functools
import math

import jax
import jax.numpy as jnp
from jax import lax
from jax.experimental import pallas as pl
from jax.experimental.pallas import tpu as pltpu

F32 = jnp.float32
BF16 = jnp.bfloat16
I32 = jnp.int32

D_MODEL = 1024
N_HEADS = 8
HEAD_DIM = 64
ATTN_W = N_HEADS * HEAD_DIM
CONV_W = 512
N_EXPERTS = 32
TOP_K = 4
D_FF = 1024
RMS_EPS = 1e-6
SWIGLU_LIMIT = 7.0
SWIGLU_ALPHA = 1.702
PAGE = 128

LANES = 128
HALO = 8
VMEM_LIMIT = 56 * 1024 * 1024

TM_IN = 512
BK = 256
TQ_ATT = 512
ATT_UNROLL = 4
PAGES_PER_STEP = 16
TM_MIX = 256
TM_MOE = 256
TM_PLE = 256


def _cparams(sem):
    return pltpu.CompilerParams(dimension_semantics=sem, vmem_limit_bytes=VMEM_LIMIT)


ROW_SUB = D_MODEL // LANES


def _store_rows(ref, x):
    rows = x.shape[0]
    for c in range(ROW_SUB):
        ref[pl.ds(c, rows, stride=ROW_SUB), :] = x[:, c * LANES:(c + 1) * LANES]


def _load_rows(ref):
    rows = ref.shape[0] // ROW_SUB
    return jnp.concatenate([ref[pl.ds(c, rows, stride=ROW_SUB), :] for c in range(ROW_SUB)], axis=1)


def _rms(x, g):
    ms = jnp.mean(x * x, axis=-1, keepdims=True)
    return x * lax.rsqrt(ms + RMS_EPS) * g


def _softplus2(z2):
    return jnp.maximum(z2, 0.0) + jnp.log2(1.0 + jnp.exp2(jnp.minimum(z2, -z2)))


def _inproj_core(x, g, w_ref, bd, qnw, knw):
    a = _rms(x, g).astype(BF16)

    def sec(i):
        return jnp.dot(a, w_ref[:, i * 512:(i + 1) * 512], preferred_element_type=F32)

    def head_norm(t, w):
        m = jnp.dot((t * t).astype(BF16), bd, preferred_element_type=F32)
        return t * lax.rsqrt(m + RMS_EPS) * w

    qn = head_norm(sec(0), qnw)
    kn = head_norm(sec(1), knw)
    v = sec(2)
    gate_b = sec(3)
    u = sec(4) * sec(5)
    return qn, kn, v, gate_b, u


def _inproj_prompt_kernel(x_ref, g_ref, w_ref, bd_ref, qnw_ref, knw_ref, cw_ref,
                          qw_ref, kt_ref, vt_ref, ktb_ref, vtb_ref, co_ref, ut_ref, ubuf):
    tm = x_ref.shape[0]

    @pl.when(pl.program_id(1) == 0)
    def _():
        ubuf[0:HALO, :] = jnp.zeros((HALO, CONV_W), F32)

    qn, kn, v, gate_b, u = _inproj_core(x_ref[...], g_ref[...], w_ref, bd_ref[...], qnw_ref[...], knw_ref[...])

    lane = lax.broadcasted_iota(I32, (1, LANES), 1)
    lo = (lane < HEAD_DIM).astype(F32)
    hi = 1.0 - lo
    for h in range(N_HEADS):
        p = h // 2
        blk = qn[:, p * LANES:(p + 1) * LANES] * (lo if h % 2 == 0 else hi)
        qw_ref[:, h * LANES:(h + 1) * LANES] = blk.astype(BF16)

    kt = kn.T
    vt = v.T
    kt_ref[...] = kt
    vt_ref[...] = vt
    for c in range(tm // BK):
        ktb_ref[c] = kt[:, c * BK:(c + 1) * BK].astype(BF16)
        vtb_ref[c] = vt[:, c * BK:(c + 1) * BK].astype(BF16)

    ubuf[HALO:HALO + tm, :] = u
    conv = (cw_ref[0:1, :] * ubuf[HALO - 2:HALO - 2 + tm, :]
            + cw_ref[1:2, :] * ubuf[HALO - 1:HALO - 1 + tm, :]
            + cw_ref[2:3, :] * u)
    co_ref[...] = (gate_b * conv).astype(BF16)
    tail = u[tm - HALO:tm, :]
    ubuf[0:HALO, :] = tail
    ut_ref[...] = tail


def _inproj_prompt(x, g, w_in, bd, qnw, knw, cw, batch, seq):
    rows = batch * seq
    nblk = seq // TM_IN
    row_map = lambda b, i: (b * nblk + i, 0)
    const = lambda b, i: (0, 0)
    out_shape = (
        jax.ShapeDtypeStruct((rows, N_HEADS * LANES), BF16),
        jax.ShapeDtypeStruct((batch, ATTN_W, seq), F32),
        jax.ShapeDtypeStruct((batch, ATTN_W, seq), F32),
        jax.ShapeDtypeStruct((batch, seq // BK, ATTN_W, BK), BF16),
        jax.ShapeDtypeStruct((batch, seq // BK, ATTN_W, BK), BF16),
        jax.ShapeDtypeStruct((rows, CONV_W), BF16),
        jax.ShapeDtypeStruct((batch, HALO, CONV_W), F32),
    )
    return pl.pallas_call(
        _inproj_prompt_kernel,
        grid=(batch, nblk),
        in_specs=[
            pl.BlockSpec((TM_IN, D_MODEL), row_map),
            pl.BlockSpec((1, D_MODEL), const),
            pl.BlockSpec(w_in.shape, const),
            pl.BlockSpec(bd.shape, const),
            pl.BlockSpec((1, ATTN_W), const),
            pl.BlockSpec((1, ATTN_W), const),
            pl.BlockSpec(cw.shape, const),
        ],
        out_specs=(
            pl.BlockSpec((TM_IN, N_HEADS * LANES), row_map),
            pl.BlockSpec((None, ATTN_W, TM_IN), lambda b, i: (b, 0, i)),
            pl.BlockSpec((None, ATTN_W, TM_IN), lambda b, i: (b, 0, i)),
            pl.BlockSpec((None, TM_IN // BK, ATTN_W, BK), lambda b, i: (b, i, 0, 0)),
            pl.BlockSpec((None, TM_IN // BK, ATTN_W, BK), lambda b, i: (b, i, 0, 0)),
            pl.BlockSpec((TM_IN, CONV_W), row_map),
            pl.BlockSpec((None, HALO, CONV_W), lambda b, i: (b, 0, 0)),
        ),
        out_shape=out_shape,
        scratch_shapes=[pltpu.VMEM((HALO + TM_IN, CONV_W), F32)],
        compiler_params=_cparams(("arbitrary", "arbitrary")),
        name="inproj_prompt",
    )(x, g, w_in, bd, qnw, knw, cw)


def _inproj_sample_kernel(x_ref, g_ref, w_ref, bd_ref, qnw_ref, knw_ref, cw_ref, h1_ref, h2_ref,
                          q_ref, k_ref, v_ref, co_ref, u_ref, ubuf, *, seq):
    rows = x_ref.shape[0]
    qn, kn, v, gate_b, u = _inproj_core(x_ref[...], g_ref[...], w_ref, bd_ref[...], qnw_ref[...], knw_ref[...])
    q_ref[...] = qn
    k_ref[...] = kn
    v_ref[...] = v
    u_ref[...] = u
    ubuf[0:HALO, :] = jnp.zeros((HALO, CONV_W), F32)
    ubuf[HALO:HALO + rows, :] = u
    t = lax.broadcasted_iota(I32, (rows, 1), 0) % seq
    x1 = jnp.where(t >= 1, ubuf[HALO - 1:HALO - 1 + rows, :], h1_ref[...])
    x2 = jnp.where(t >= 2, ubuf[HALO - 2:HALO - 2 + rows, :], h2_ref[...])
    conv = cw_ref[0:1, :] * x2 + cw_ref[1:2, :] * x1 + cw_ref[2:3, :] * u
    co_ref[...] = (gate_b * conv).astype(BF16)


def _inproj_sample(x, g, w_in, bd, qnw, knw, cw, halo1, halo2, seq):
    rows = x.shape[0]
    full = lambda a: pl.BlockSpec(a.shape, lambda i: (0,) * a.ndim)
    args = (x, g, w_in, bd, qnw, knw, cw, halo1, halo2)
    shp = lambda w, dt: jax.ShapeDtypeStruct((rows, w), dt)
    out_shape = (shp(ATTN_W, F32), shp(ATTN_W, F32), shp(ATTN_W, F32), shp(CONV_W, BF16), shp(CONV_W, F32))
    return pl.pallas_call(
        functools.partial(_inproj_sample_kernel, seq=seq),
        grid=(1,),
        in_specs=[full(a) for a in args],
        out_specs=tuple(pl.BlockSpec(s.shape, lambda i: (0, 0)) for s in out_shape),
        out_shape=out_shape,
        scratch_shapes=[pltpu.VMEM((HALO + rows, CONV_W), F32)],
        compiler_params=_cparams(("arbitrary",)),
        name="inproj_sample",
    )(*args)


def _attn_prompt_kernel(bias_ref, q_ref, kt_ref, vt_ref, u_ref, o_ref):
    p = pl.program_id(1)
    i = pl.program_id(2)
    tq = q_ref.shape[0]
    kq = tq // BK
    row = lax.broadcasted_iota(I32, (2 * tq, 1), 0)
    second = row >= tq
    bias = jnp.where(second, bias_ref[2 * p + 1], bias_ref[2 * p])
    qry = jnp.where(second, row - tq, row)
    key = lax.broadcasted_iota(I32, (2 * tq, BK), 1)
    feat = lax.broadcasted_iota(I32, (LANES, 2 * BK), 0)
    half = lax.broadcasted_iota(I32, (LANES, 2 * BK), 1) >= BK
    own = (feat >= HEAD_DIM) == half
    u = u_ref[...]
    lane = lax.broadcasted_iota(I32, (1, LANES), 1)
    b_hi = bias.astype(BF16).astype(F32)
    b_cols = jnp.where(lane == 0, b_hi, jnp.where(lane == 1, bias - b_hi, 0.0)).astype(BF16)
    q2 = jnp.concatenate([q_ref[:, 0:LANES], q_ref[:, LANES:2 * LANES]], axis=0)
    q2 = jnp.concatenate([q2, b_cols], axis=1)
    ones_rows = (lax.broadcasted_iota(I32, (LANES, BK), 0) < 2).astype(BF16)

    def tile(j, carry, acc, causal=None):
        kt = jnp.concatenate([kt_ref[j], ones_rows], axis=0)
        vt = vt_ref[j]
        z = jnp.dot(q2, kt, preferred_element_type=F32)
        sp = _softplus2(z)
        if causal is not None:
            sp = jnp.where(causal, sp, 0.0)
        later = jnp.dot(sp.astype(BF16), u, preferred_element_type=F32)
        w = jnp.exp2(z - sp - later + carry)
        if causal is not None:
            w = jnp.where(causal, w, 0.0)
        wb = w.astype(BF16)
        w2 = jnp.concatenate([wb[:tq], wb[tq:]], axis=1)
        v2 = jnp.concatenate([vt, vt], axis=1)
        v2 = jnp.where(own, v2, jnp.zeros_like(v2))
        acc = acc + lax.dot_general(w2, v2, (((1,), (1,)), ((), ())), preferred_element_type=F32)
        carry = carry - jnp.sum(sp, axis=1, keepdims=True)
        return carry, acc

    state = (jnp.zeros((2 * tq, 1), F32), jnp.zeros((tq, LANES), F32))
    for d in reversed(range(kq)):
        state = tile(kq * i + d, state[0], state[1], key + d * BK < qry)

    def run(j_top, n, c):
        for k in range(n):
            c = tile(j_top - k, c[0], c[1])
        return c

    n_below = kq * i
    rem = n_below % ATT_UNROLL
    state = lax.cond(rem != 0, lambda c: run(n_below - 1, kq, c), lambda c: c, state)
    top = n_below - rem
    carry, acc = lax.fori_loop(0, n_below // ATT_UNROLL,
                               lambda it, c: run(top - 1 - ATT_UNROLL * it, ATT_UNROLL, c), state)
    o_ref[...] = acc.astype(BF16)


def _attn_prompt(bias2, q_wide, ktb, vtb, u, batch, seq):
    assert ATT_UNROLL == 2 * (TQ_ATT // BK), "the remainder of the unrolled key loop is one query block's tiles"
    nq = seq // TQ_ATT
    kv_spec = pl.BlockSpec((None, seq // BK, LANES, BK), lambda b, p, i: (b, 0, p, 0))
    return pl.pallas_call(
        _attn_prompt_kernel,
        grid=(batch, N_HEADS // 2, nq),
        in_specs=[
            pl.BlockSpec(memory_space=pltpu.SMEM),
            pl.BlockSpec((TQ_ATT, 2 * LANES), lambda b, p, i: (b * nq + i, p)),
            kv_spec,
            kv_spec,
            pl.BlockSpec((BK, BK), lambda b, p, i: (0, 0)),
        ],
        out_specs=pl.BlockSpec((TQ_ATT, LANES), lambda b, p, i: (b * nq + i, p)),
        out_shape=jax.ShapeDtypeStruct((batch * seq, ATTN_W), BF16),
        compiler_params=_cparams(("arbitrary", "arbitrary", "arbitrary")),
        name="attn_prompt",
    )(bias2, q_wide, ktb, vtb, u)


def _attn_sample_kernel(pt_ref, bias_ref, q_ref, kn_ref, vn_ref, *rest, seq):
    del pt_ref
    npg = PAGES_PER_STEP
    k_refs = rest[:npg]
    v_refs = rest[npg:2 * npg]
    ucat_ref, bm_ref, o_ref, qbd, acc, carry = rest[2 * npg:]
    j = pl.program_id(1)
    rows = N_HEADS * seq

    def page(ktp, vtp, masked):
        z = jnp.dot(qbd[...], ktp, preferred_element_type=F32) + bias_ref[...]
        sp = _softplus2(z)
        if masked:
            t = lax.broadcasted_iota(I32, (rows, PAGE), 0) % seq
            s = lax.broadcasted_iota(I32, (rows, PAGE), 1)
            m = s < t
            sp = jnp.where(m, sp, 0.0)
        lt = jnp.dot(sp.astype(BF16), ucat_ref[...], preferred_element_type=F32)
        w = jnp.exp2(z - sp - lt[:, :PAGE] + carry[...])
        if masked:
            w = jnp.where(m, w, 0.0)
        acc[...] += lax.dot_general(w.astype(BF16), vtp, (((1,), (1,)), ((), ())), preferred_element_type=F32)
        carry[...] -= lt[:, PAGE:]

    @pl.when(j == 0)
    def _():
        q8 = jnp.concatenate([q_ref[...]] * N_HEADS, axis=0)
        qbd[...] = (q8 * bm_ref[...]).astype(BF16)
        acc[...] = jnp.zeros_like(acc)
        carry[...] = jnp.zeros_like(carry)
        page(kn_ref[...], vn_ref[...], True)

    ktc = jnp.concatenate([r[...].reshape(ATTN_W, PAGE).astype(BF16) for r in k_refs], axis=1)
    vtc = jnp.concatenate([r[...].reshape(ATTN_W, PAGE).astype(BF16) for r in v_refs], axis=1)
    zc = jnp.dot(qbd[...], ktc, preferred_element_type=F32)
    run = carry[...]
    ws = []
    for i in range(npg):
        z = zc[:, i * PAGE:(i + 1) * PAGE] + bias_ref[...]
        sp = _softplus2(z)
        lt = jnp.dot(sp.astype(BF16), ucat_ref[...], preferred_element_type=F32)
        ws.append(jnp.exp2(z - sp - lt[:, :PAGE] + run).astype(BF16))
        run = run - lt[:, PAGE:]
    carry[...] = run
    acc[...] += lax.dot_general(jnp.concatenate(ws, axis=1), vtc, (((1,), (1,)), ((), ())),
                                preferred_element_type=F32)

    @pl.when(j == pl.num_programs(1) - 1)
    def _():
        d = acc[...] * bm_ref[...]
        out = d[0:seq, :]
        for h in range(1, N_HEADS):
            out = out + d[h * seq:(h + 1) * seq, :]
        o_ref[...] = out


def _attn_sample(page_table, bias_rows, q_s, kn_pad, vn_pad, cache_k, cache_v, ucat, bm, seq):
    dec_b, n_pages = page_table.shape
    npg = PAGES_PER_STEP
    steps = n_pages // npg

    def page_spec(i):
        return pl.BlockSpec((None, N_HEADS, HEAD_DIM, PAGE),
                            lambda b, j, pt, i=i: (pt[b, n_pages - 1 - (j * npg + i)], 0, 0, 0))

    const2 = lambda b, j, pt: (0, 0)
    per_seq = lambda b, j, pt: (b, 0, 0)
    in_specs = [
        pl.BlockSpec(bias_rows.shape, const2),
        pl.BlockSpec((None, seq, ATTN_W), per_seq),
        pl.BlockSpec((None, ATTN_W, PAGE), per_seq),
        pl.BlockSpec((None, ATTN_W, PAGE), per_seq),
    ] + [page_spec(i) for i in range(npg)] + [page_spec(i) for i in range(npg)] + [
        pl.BlockSpec(ucat.shape, const2),
        pl.BlockSpec(bm.shape, const2),
    ]
    return pl.pallas_call(
        functools.partial(_attn_sample_kernel, seq=seq),
        grid_spec=pltpu.PrefetchScalarGridSpec(
            num_scalar_prefetch=1,
            grid=(dec_b, steps),
            in_specs=in_specs,
            out_specs=pl.BlockSpec((None, seq, ATTN_W), per_seq),
            scratch_shapes=[pltpu.VMEM((N_HEADS * seq, ATTN_W), BF16),
                            pltpu.VMEM((N_HEADS * seq, ATTN_W), F32),
                            pltpu.VMEM((N_HEADS * seq, PAGE), F32)],
        ),
        out_shape=jax.ShapeDtypeStruct((dec_b, seq, ATTN_W), F32),
        compiler_params=_cparams(("arbitrary", "arbitrary")),
        name="attn_sample",
    )(page_table, bias_rows, q_s, kn_pad, vn_pad, *([cache_k] * npg), *([cache_v] * npg), ucat, bm)


def _mixout_kernel(hp_ref, atp_ref, cop_ref, hs_ref, ats_ref, cos_ref, woa_ref, woc_ref, g_ref, wr1_ref, wr2_ref,
                   br_ref, ltri_ref, h1_ref, xn_ref, ti_ref, tw_ref, rk_ref, cnt_ref, cnt_sc, *, np_blk):
    tm = hp_ref.shape[0]
    i = pl.program_id(0)

    @pl.when(i == 0)
    def _():
        cnt_sc[...] = jnp.zeros_like(cnt_sc)

    is_s = i >= np_blk
    h = jnp.where(is_s, hs_ref[...], hp_ref[...])
    at = jnp.where(is_s, ats_ref[...], atp_ref[...])
    co = jnp.where(is_s, cos_ref[...], cop_ref[...])
    h1 = (h + jnp.dot(at, woa_ref[...], preferred_element_type=F32)
          + jnp.dot(co, woc_ref[...], preferred_element_type=F32))
    h1_ref[...] = h1
    xn = _rms(h1, g_ref[...])
    _store_rows(xn_ref, xn)

    xh = xn.astype(BF16)
    xl = (xn - xh.astype(F32)).astype(BF16)
    t1 = jnp.dot(xh, wr1_ref[...], preferred_element_type=F32)
    t2 = jnp.dot(xl, wr2_ref[...], preferred_element_type=F32)
    logits = t1[:, :LANES] + t1[:, LANES:] + t2 + br_ref[...]

    lane_i = lax.broadcasted_iota(I32, (tm, LANES), 1)
    lane = lane_i.astype(F32)
    cur = logits
    vals, idxs, hots = [], [], []
    for _ in range(TOP_K):
        m = jnp.max(cur, axis=-1, keepdims=True)
        idx = jnp.min(jnp.where(cur == m, lane, float(LANES)), axis=-1, keepdims=True)
        hot = lane == idx
        cur = jnp.where(hot, -jnp.inf, cur)
        vals.append(m)
        idxs.append(idx)
        hots.append(hot)
    es = [jnp.exp(v - vals[0]) for v in vals]
    denom = es[0] + es[1] + es[2] + es[3]

    member = hots[0] | hots[1] | hots[2] | hots[3]
    mf = member.astype(F32)
    before = jnp.dot(ltri_ref[...], mf.astype(BF16), preferred_element_type=F32) + cnt_sc[...]
    cnt_sc[...] += jnp.sum(mf, axis=0, keepdims=True)
    cnt_ref[...] = cnt_sc[...]

    ti = jnp.zeros((tm, LANES), I32)
    tw = jnp.zeros((tm, LANES), F32)
    rk = jnp.zeros((tm, LANES), I32)
    for c in range(TOP_K):
        sel = lane_i == c
        rank = jnp.sum(jnp.where(hots[c], before, 0.0), axis=-1, keepdims=True)
        ti = jnp.where(sel, idxs[c].astype(I32), ti)
        tw = jnp.where(sel, es[c] / denom, tw)
        rk = jnp.where(sel, rank.astype(I32), rk)
    ti_ref[...] = ti
    tw_ref[...] = tw
    rk_ref[...] = rk


def _mixout(hp, atp, cop, hs, ats, cos, woa, woc, g, wr1, wr2, br):
    tm = TM_MIX
    np_blk = hp.shape[0] // tm
    ns_blk = hs.shape[0] // tm
    n_all = hp.shape[0] + hs.shape[0]
    ltri = jnp.tril(jnp.ones((tm, tm), F32), -1).astype(BF16)
    const = lambda i: (0, 0)
    row_p = lambda i: (jnp.minimum(i, np_blk - 1), 0)
    row_s = lambda i: (jnp.maximum(i - np_blk, 0), 0)
    row_out = lambda i: (i, 0)
    ins = [hp, atp, cop, hs, ats, cos, woa, woc, g, wr1, wr2, br, ltri]
    in_specs = [
        pl.BlockSpec((tm, D_MODEL), row_p),
        pl.BlockSpec((tm, ATTN_W), row_p),
        pl.BlockSpec((tm, CONV_W), row_p),
        pl.BlockSpec((tm, D_MODEL), row_s),
        pl.BlockSpec((tm, ATTN_W), row_s),
        pl.BlockSpec((tm, CONV_W), row_s),
        pl.BlockSpec(woa.shape, const),
        pl.BlockSpec(woc.shape, const),
        pl.BlockSpec(g.shape, const),
        pl.BlockSpec(wr1.shape, const),
        pl.BlockSpec(wr2.shape, const),
        pl.BlockSpec(br.shape, const),
        pl.BlockSpec(ltri.shape, const),
    ]
    out_shape = (
        jax.ShapeDtypeStruct((n_all, D_MODEL), F32),
        jax.ShapeDtypeStruct((n_all * ROW_SUB, LANES), F32),
        jax.ShapeDtypeStruct((n_all, LANES), I32),
        jax.ShapeDtypeStruct((n_all, LANES), F32),
        jax.ShapeDtypeStruct((n_all, LANES), I32),
    )
    outs = pl.pallas_call(
        functools.partial(_mixout_kernel, np_blk=np_blk),
        grid=(np_blk + ns_blk,),
        in_specs=in_specs,
        out_specs=(
            pl.BlockSpec((tm, D_MODEL), row_out),
            pl.BlockSpec((tm * ROW_SUB, LANES), row_out),
            pl.BlockSpec((tm, LANES), row_out),
            pl.BlockSpec((tm, LANES), row_out),
            pl.BlockSpec((tm, LANES), row_out),
            pl.BlockSpec((1, LANES), const),
        ),
        out_shape=out_shape + (jax.ShapeDtypeStruct((1, LANES), F32),),
        scratch_shapes=[pltpu.VMEM((1, LANES), F32)],
        compiler_params=_cparams(("arbitrary",)),
        name="mixout",
    )(*ins)
    return outs[:5], outs[5]


def _dispatch_kernel(pad_lo_ref, pad_n_ref, dst_ref, x_ref, xs_hbm, zero, sem, zsem):
    t = pl.program_id(0)
    tm = dst_ref.shape[2] // TOP_K

    def rows(ref, i):
        return ref.at[pl.ds(pl.multiple_of(i * ROW_SUB, ROW_SUB), ROW_SUB)]

    @pl.when(t == 0)
    def _():
        zero[...] = jnp.zeros_like(zero)
        for e in range(pad_n_ref.shape[0]):
            def fill(k, carry, e=e):
                pltpu.make_async_copy(zero, rows(xs_hbm, pad_lo_ref[e] + k), zsem.at[0]).start()
                return carry
            lax.fori_loop(0, pad_n_ref[e], fill, 0)

    def issue(r, carry):
        for c in range(TOP_K):
            pltpu.make_async_copy(rows(x_ref, r), rows(xs_hbm, dst_ref[0, 0, r * TOP_K + c]), sem.at[0]).start()
        return carry
    lax.fori_loop(0, tm, issue, 0, unroll=4)

    def drain(r, carry):
        for c in range(TOP_K):
            pltpu.make_async_copy(rows(x_ref, 0), rows(xs_hbm, 0), sem.at[0]).wait()
        return carry
    lax.fori_loop(0, tm, drain, 0, unroll=4)

    @pl.when(t == pl.num_programs(0) - 1)
    def _():
        for e in range(pad_n_ref.shape[0]):
            def unfill(k, carry):
                pltpu.make_async_copy(zero, rows(xs_hbm, 0), zsem.at[0]).wait()
                return carry
            lax.fori_loop(0, pad_n_ref[e], unfill, 0)


def _dispatch(pad_lo, pad_n, dest_blk, xn_all, n_slots):
    nblk = dest_blk.shape[0]
    return pl.pallas_call(
        _dispatch_kernel,
        grid_spec=pltpu.PrefetchScalarGridSpec(
            num_scalar_prefetch=2,
            grid=(nblk,),
            in_specs=[
                pl.BlockSpec((1, 1, dest_blk.shape[2]), lambda t, lo, n: (t, 0, 0), memory_space=pltpu.SMEM),
                pl.BlockSpec((dest_blk.shape[2] // TOP_K * ROW_SUB, LANES), lambda t, lo, n: (t, 0)),
            ],
            out_specs=pl.BlockSpec(memory_space=pl.ANY),
            scratch_shapes=[pltpu.VMEM((ROW_SUB, LANES), F32),
                            pltpu.SemaphoreType.DMA((1,)),
                            pltpu.SemaphoreType.DMA((1,))],
        ),
        out_shape=jax.ShapeDtypeStruct((n_slots * ROW_SUB, LANES), F32),
        compiler_params=_cparams(("arbitrary",)),
        name="dispatch",
    )(pad_lo, pad_n, dest_blk, xn_all)


def _moe_kernel(te_ref, nu_ref, x_ref, wgu_ref, bgu_ref, wd_ref, bd_ref, y_ref, wgu_bf, wd_bf):
    t = pl.program_id(0)
    n_used = nu_ref[0]

    @pl.when(t >= n_used)
    def _():
        y_ref[...] = jnp.zeros_like(y_ref)

    @pl.when(t < n_used)
    def _():
        prev_e = te_ref[jnp.maximum(t - 1, 0)]

        @pl.when((t == 0) | (te_ref[t] != prev_e))
        def _():
            wgu_bf[...] = wgu_ref[...].astype(BF16)
            wd_bf[...] = wd_ref[...].astype(BF16)

        x = _load_rows(x_ref).astype(BF16)
        hh = jnp.dot(x, wgu_bf[...], preferred_element_type=F32) + bgu_ref[...]
        g = jnp.minimum(hh[:, :D_FF], SWIGLU_LIMIT)
        u = jnp.clip(hh[:, D_FF:], -SWIGLU_LIMIT, SWIGLU_LIMIT)
        act = (u + 1.0) * g * jax.nn.sigmoid(SWIGLU_ALPHA * g)
        _store_rows(y_ref, jnp.dot(act.astype(BF16), wd_bf[...], preferred_element_type=F32) + bd_ref[...])


def _moe(tile_expert, n_used, x_sorted, wgu, bgu, wd, bd):
    tm = TM_MOE
    n_tiles = x_sorted.shape[0] // (tm * ROW_SUB)
    e_map = lambda t, te, nu: (te[t], 0, 0)
    return pl.pallas_call(
        _moe_kernel,
        grid_spec=pltpu.PrefetchScalarGridSpec(
            num_scalar_prefetch=2,
            grid=(n_tiles,),
            in_specs=[
                pl.BlockSpec((tm * ROW_SUB, LANES), lambda t, te, nu: (jnp.minimum(t, nu[0] - 1), 0)),
                pl.BlockSpec((None, D_MODEL, 2 * D_FF), e_map),
                pl.BlockSpec((None, 1, 2 * D_FF), e_map),
                pl.BlockSpec((None, D_FF, D_MODEL), e_map),
                pl.BlockSpec((None, 1, D_MODEL), e_map),
            ],
            out_specs=pl.BlockSpec((tm * ROW_SUB, LANES), lambda t, te, nu: (t, 0)),
            scratch_shapes=[pltpu.VMEM((D_MODEL, 2 * D_FF), BF16),
                            pltpu.VMEM((D_FF, D_MODEL), BF16)],
        ),
        out_shape=jax.ShapeDtypeStruct(x_sorted.shape, F32),
        compiler_params=_cparams(("arbitrary",)),
        name="moe",
    )(tile_expert, n_used, x_sorted, wgu, bgu, wd, bd)


def _ple_kernel(dst_cur, dst_nxt, y_hbm, h1_ref, tw_ref, p_ref, g_ref, wg_ref, wp_ref, o_ref, ybuf, sem):
    t = pl.program_id(0)
    nt = pl.num_programs(0)
    slot = t % 2
    tm = o_ref.shape[0]

    def row_copy(row, c, r, s):
        return pltpu.make_async_copy(y_hbm.at[pl.ds(pl.multiple_of(row * ROW_SUB, ROW_SUB), ROW_SUB)],
                                     ybuf.at[s, c, pl.ds(pl.multiple_of(r * ROW_SUB, ROW_SUB), ROW_SUB)], sem.at[s])

    def gather(dst, s):
        def body(r, carry):
            for c in range(TOP_K):
                row_copy(dst[0, 0, r * TOP_K + c], c, r, s).start(priority=c % 2)
            return carry
        lax.fori_loop(0, tm, body, 0, unroll=4)

    def drain(s):
        def body(r, carry):
            for c in range(TOP_K):
                row_copy(0, c, r, s).wait()
            return carry
        lax.fori_loop(0, tm, body, 0, unroll=4)

    @pl.when(t == 0)
    def _():
        gather(dst_cur, 0)

    @pl.when(t + 1 < nt)
    def _():
        gather(dst_nxt, 1 - slot)

    drain(slot)
    tw = tw_ref[...]
    moe = tw[:, 0:1] * _load_rows(ybuf.at[slot, 0])
    for c in range(1, TOP_K):
        moe = moe + tw[:, c:c + 1] * _load_rows(ybuf.at[slot, c])
    h2 = h1_ref[...] + moe
    gate = jax.nn.sigmoid(jnp.dot(_rms(h2, g_ref[...]).astype(BF16), wg_ref[...], preferred_element_type=F32))
    proj = jnp.dot(p_ref[...].astype(BF16), wp_ref[...], preferred_element_type=F32)
    o_ref[...] = h2 + gate * proj


def _ple(dest, y_sorted, h1_all, tw_all, p, g, wg, wp, row_off):
    rows = p.shape[0]
    tm = min(TM_PLE, rows)
    nblk = rows // tm
    blk_off = row_off // tm
    last = nblk - 1
    const = lambda t: (0, 0)
    return pl.pallas_call(
        _ple_kernel,
        grid=(nblk,),
        in_specs=[
            pl.BlockSpec((1, 1, tm * TOP_K), lambda t: (t + blk_off, 0, 0), memory_space=pltpu.SMEM),
            pl.BlockSpec((1, 1, tm * TOP_K), lambda t: (jnp.minimum(t + 1, last) + blk_off, 0, 0),
                         memory_space=pltpu.SMEM),
            pl.BlockSpec(memory_space=pl.ANY),
            pl.BlockSpec((tm, D_MODEL), lambda t: (t + blk_off, 0)),
            pl.BlockSpec((tm, LANES), lambda t: (t + blk_off, 0)),
            pl.BlockSpec((tm, p.shape[1]), lambda t: (t, 0)),
            pl.BlockSpec(g.shape, const),
            pl.BlockSpec(wg.shape, const),
            pl.BlockSpec(wp.shape, const),
        ],
        out_specs=pl.BlockSpec((tm, D_MODEL), lambda t: (t, 0)),
        out_shape=jax.ShapeDtypeStruct((rows, D_MODEL), F32),
        scratch_shapes=[pltpu.VMEM((2, TOP_K, tm * ROW_SUB, LANES), F32), pltpu.SemaphoreType.DMA((2,))],
        compiler_params=_cparams(("arbitrary",)),
        name="ple_%d" % rows,
    )(dest, dest, y_sorted, h1_all, tw_all, p, g, wg, wp)


def kernel(x_prompt, x_sample, p_prompt, p_sample, cache_k, cache_v, state_conv, page_table, g_mix, w_in,
           q_norm_w, k_norm_w, sb_bias, conv_w, w_out, g_moe, w_router, b_router, w_gate_up, b_gate_up, w_down,
           b_down, g_ple, w_ple_gate, w_ple_proj):
    depth = g_mix.shape[0]
    assert depth == 1, "single-layer step"
    batch, seq, _ = x_prompt.shape
    dec_b, dec_seq, _ = x_sample.shape
    n_pool = cache_k.shape[1]
    n_p = batch * seq
    n_s = dec_b * dec_seq
    n_all = n_p + n_s
    assert seq % TM_IN == 0 and TM_IN % BK == 0 and n_p % TM_MIX == 0 and n_s % TM_MIX == 0
    assert seq % TQ_ATT == 0 and TQ_ATT % BK == 0
    assert n_p % TM_PLE == 0 and n_s % TM_PLE == 0
    assert page_table.shape[1] % PAGES_PER_STEP == 0 and cache_k.shape[2] == PAGE

    log2e = math.log2(math.e)
    scale2 = log2e / math.sqrt(HEAD_DIM)
    bias2 = sb_bias[0] * log2e
    w_in_bf = w_in[0].astype(BF16)
    head_id = jnp.arange(ATTN_W) // HEAD_DIM
    bd = jnp.where(head_id[:, None] == head_id[None, :], 1.0 / HEAD_DIM, 0.0).astype(BF16)
    qnw = (jnp.tile(q_norm_w[0], N_HEADS) * scale2)[None, :]
    knw = jnp.tile(k_norm_w[0], N_HEADS)[None, :]
    g_mix2 = g_mix[0][None, :]
    cw = conv_w[0]
    woa = w_out[0, :ATTN_W].astype(BF16)
    woc = w_out[0, ATTN_W:].astype(BF16)
    wr = jnp.pad(w_router[0], ((0, 0), (0, LANES - N_EXPERTS)))
    wr_hi = wr.astype(BF16)
    wr_lo = (wr - wr_hi.astype(F32)).astype(BF16)
    wr1 = jnp.concatenate([wr_hi, wr_lo], axis=1)
    br = jnp.concatenate([b_router[0], jnp.full((LANES - N_EXPERTS,), -1e30, F32)])[None, :]
    g_moe2 = g_moe[0][None, :]
    g_ple2 = g_ple[0][None, :]
    wpg = w_ple_gate[0].astype(BF16)
    wpp = w_ple_proj[0].astype(BF16)

    xp = x_prompt.reshape(n_p, D_MODEL)
    xs = x_sample.reshape(n_s, D_MODEL)
    qw_p, kt_p, vt_p, ktb_p, vtb_p, co_p, ut_p = _inproj_prompt(xp, g_mix2, w_in_bf, bd, qnw, knw, cw, batch, seq)
    st = state_conv[0]
    zero_s = jnp.zeros((dec_b, dec_seq, CONV_W), F32)
    halo1 = zero_s.at[:, 0].set(st[:, 1]).reshape(n_s, CONV_W)
    halo2 = zero_s.at[:, 0].set(st[:, 0]).at[:, 1].set(st[:, 1]).reshape(n_s, CONV_W)
    q_s, k_s, v_s, co_s, u_s = _inproj_sample(xs, g_mix2, w_in_bf, bd, qnw, knw, cw, halo1, halo2, dec_seq)

    u_tri = jnp.tril(jnp.ones((BK, BK), F32), -1).astype(BF16)
    attn_p = _attn_prompt(bias2, qw_p, ktb_p, vtb_p, u_tri, batch, seq)

    def new_page(a):
        at = jnp.swapaxes(a.reshape(dec_b, dec_seq, ATTN_W), 1, 2).astype(BF16)
        return jnp.pad(at, ((0, 0), (0, 0), (0, PAGE - dec_seq)))

    row_head = jnp.arange(N_HEADS * dec_seq) // dec_seq
    bias_rows = jnp.broadcast_to(bias2[row_head][:, None], (N_HEADS * dec_seq, PAGE)).astype(F32)
    bm = (row_head[:, None] == head_id[None, :]).astype(F32)
    ucat = jnp.concatenate([jnp.tril(jnp.ones((PAGE, PAGE), F32), -1), jnp.ones((PAGE, PAGE), F32)],
                           axis=1).astype(BF16)
    ck = jnp.transpose(cache_k[0], (0, 2, 3, 1))
    cv = jnp.transpose(cache_v[0], (0, 2, 3, 1))
    attn_s = _attn_sample(page_table, bias_rows, q_s.reshape(dec_b, dec_seq, ATTN_W), new_page(k_s), new_page(v_s),
                          ck, cv, ucat, bm, dec_seq)

    bufs, cnt = _mixout(xp, attn_p, co_p, xs, attn_s.reshape(n_s, ATTN_W).astype(BF16), co_s, woa, woc, g_moe2,
                        wr1, wr_hi, br)
    h1_all, xn_all, ti_all, tw_all, rk_all = bufs

    n_tiles = (n_all * TOP_K + N_EXPERTS * (TM_MOE - 1)) // TM_MOE + 1
    counts = cnt[0, :N_EXPERTS].astype(I32)
    padded = ((counts + TM_MOE - 1) // TM_MOE) * TM_MOE
    ends = jnp.cumsum(padded)
    starts = ends - padded
    top_i = ti_all[:, :TOP_K]
    dest = starts[top_i] + rk_all[:, :TOP_K]
    tile_start = jnp.arange(n_tiles, dtype=I32) * TM_MOE
    tile_expert = jnp.minimum(jnp.sum((tile_start[:, None] >= ends[None, :]).astype(I32), axis=1), N_EXPERTS - 1)
    n_used = (ends[-1] // TM_MOE).astype(I32)[None]
    dest_blk = dest.reshape(n_all // TM_PLE, 1, TM_PLE * TOP_K)

    n_slots = n_tiles * TM_MOE
    pad_lo = jnp.concatenate([starts + counts, ends[-1:]])
    pad_n = jnp.concatenate([padded - counts, n_slots - ends[-1:]])
    x_sorted = _dispatch(pad_lo, pad_n, dest_blk, xn_all, n_slots)
    y_sorted = _moe(tile_expert, n_used, x_sorted, w_gate_up[0], b_gate_up[0][:, None, :], w_down[0],
                    b_down[0][:, None, :])

    y_p = _ple(dest_blk, y_sorted, h1_all, tw_all, p_prompt[0].reshape(n_p, -1), g_ple2, wpg, wpp, 0)
    y_s = _ple(dest_blk, y_sorted, h1_all, tw_all, p_sample[0].reshape(n_s, -1), g_ple2, wpg, wpp, n_p)

    new_conv_p = ut_p[:, HALO - 2:, :]
    new_conv_s = u_s.reshape(dec_b, dec_seq, CONV_W)[:, dec_seq - 2:, :]
    return (
        y_p.reshape(batch, seq, D_MODEL),
        y_s.reshape(dec_b, dec_seq, D_MODEL),
        jnp.transpose(kt_p.reshape(batch, N_HEADS, HEAD_DIM, seq), (0, 3, 1, 2))[None],
        jnp.transpose(vt_p.reshape(batch, N_HEADS, HEAD_DIM, seq), (0, 3, 1, 2))[None],
        new_conv_p[None],
        k_s.reshape(1, dec_b, dec_seq, N_HEADS, HEAD_DIM),
        v_s.reshape(1, dec_b, dec_seq, N_HEADS, HEAD_DIM),
        new_conv_s[None],
    )
```

```python
import functools
import math

import jax
import jax.numpy as jnp
from jax import lax
from jax.experimental import pallas as pl
from jax.experimental.pallas import tpu as pltpu

F32 = jnp.float32
BF16 = jnp.bfloat16
I32 = jnp.int32

D_MODEL = 1024
N_HEADS = 8
HEAD_DIM = 64
ATTN_W = N_HEADS * HEAD_DIM
CONV_W = 512
N_EXPERTS = 32
TOP_K = 4
D_FF = 1024
RMS_EPS = 1e-6
SWIGLU_LIMIT = 7.0
SWIGLU_ALPHA = 1.702
PAGE = 128

LANES = 128
HALO = 8
VMEM_LIMIT = 56 * 1024 * 1024

TM_IN = 512
BK = 256
TQ_ATT = 512
ATT_UNROLL = 4
PAGES_PER_STEP = 16
TM_MIX = 256
TM_MOE = 256
ZERO_SLOTS = 16
TM_PLE = 256


def _cparams(sem):
    return pltpu.CompilerParams(dimension_semantics=sem, vmem_limit_bytes=VMEM_LIMIT)


ROW_SUB = D_MODEL // LANES


def _store_rows(ref, x):
    rows = x.shape[0]
    for c in range(ROW_SUB):
        ref[pl.ds(c, rows, stride=ROW_SUB), :] = x[:, c * LANES:(c + 1) * LANES]


def _load_rows(ref):
    rows = ref.shape[0] // ROW_SUB
    return jnp.concatenate([ref[pl.ds(c, rows, stride=ROW_SUB), :] for c in range(ROW_SUB)], axis=1)


def _rms(x, g):
    ms = jnp.mean(x * x, axis=-1, keepdims=True)
    return x * lax.rsqrt(ms + RMS_EPS) * g


def _softplus2(z2):
    return jnp.maximum(z2, 0.0) + jnp.log2(1.0 + jnp.exp2(jnp.minimum(z2, -z2)))


def _inproj_core(x, g, w_ref, bd, qnw, knw):
    a = _rms(x, g).astype(BF16)

    def sec(i):
        return jnp.dot(a, w_ref[:, i * 512:(i + 1) * 512], preferred_element_type=F32)

    def head_norm(t, w):
        m = jnp.dot((t * t).astype(BF16), bd, preferred_element_type=F32)
        return t * lax.rsqrt(m + RMS_EPS) * w

    qn = head_norm(sec(0), qnw)
    kn = head_norm(sec(1), knw)
    v = sec(2)
    gate_b = sec(3)
    u = sec(4) * sec(5)
    return qn, kn, v, gate_b, u


def _inproj_prompt_kernel(x_ref, g_ref, w_ref, bd_ref, qnw_ref, knw_ref, cw_ref,
                          qw_ref, kt_ref, vt_ref, ktb_ref, vtb_ref, co_ref, ut_ref, ubuf):
    tm = x_ref.shape[0]

    @pl.when(pl.program_id(1) == 0)
    def _():
        ubuf[0:HALO, :] = jnp.zeros((HALO, CONV_W), F32)

    qn, kn, v, gate_b, u = _inproj_core(x_ref[...], g_ref[...], w_ref, bd_ref[...], qnw_ref[...], knw_ref[...])

    lane = lax.broadcasted_iota(I32, (1, LANES), 1)
    lo = (lane < HEAD_DIM).astype(F32)
    hi = 1.0 - lo
    for h in range(N_HEADS):
        p = h // 2
        blk = qn[:, p * LANES:(p + 1) * LANES] * (lo if h % 2 == 0 else hi)
        qw_ref[:, h * LANES:(h + 1) * LANES] = blk.astype(BF16)

    kt = kn.T
    vt = v.T
    kt_ref[...] = kt
    vt_ref[...] = vt
    for c in range(tm // BK):
        ktb_ref[c] = kt[:, c * BK:(c + 1) * BK].astype(BF16)
        vtb_ref[c] = vt[:, c * BK:(c + 1) * BK].astype(BF16)

    ubuf[HALO:HALO + tm, :] = u
    conv = (cw_ref[0:1, :] * ubuf[HALO - 2:HALO - 2 + tm, :]
            + cw_ref[1:2, :] * ubuf[HALO - 1:HALO - 1 + tm, :]
            + cw_ref[2:3, :] * u)
    co_ref[...] = (gate_b * conv).astype(BF16)
    tail = u[tm - HALO:tm, :]
    ubuf[0:HALO, :] = tail
    ut_ref[...] = tail


def _inproj_prompt(x, g, w_in, bd, qnw, knw, cw, batch, seq):
    rows = batch * seq
    nblk = seq // TM_IN
    row_map = lambda b, i: (b * nblk + i, 0)
    const = lambda b, i: (0, 0)
    out_shape = (
        jax.ShapeDtypeStruct((rows, N_HEADS * LANES), BF16),
        jax.ShapeDtypeStruct((batch, ATTN_W, seq), F32),
        jax.ShapeDtypeStruct((batch, ATTN_W, seq), F32),
        jax.ShapeDtypeStruct((batch, seq // BK, ATTN_W, BK), BF16),
        jax.ShapeDtypeStruct((batch, seq // BK, ATTN_W, BK), BF16),
        jax.ShapeDtypeStruct((rows, CONV_W), BF16),
        jax.ShapeDtypeStruct((batch, HALO, CONV_W), F32),
    )
    return pl.pallas_call(
        _inproj_prompt_kernel,
        grid=(batch, nblk),
        in_specs=[
            pl.BlockSpec((TM_IN, D_MODEL), row_map),
            pl.BlockSpec((1, D_MODEL), const),
            pl.BlockSpec(w_in.shape, const),
            pl.BlockSpec(bd.shape, const),
            pl.BlockSpec((1, ATTN_W), const),
            pl.BlockSpec((1, ATTN_W), const),
            pl.BlockSpec(cw.shape, const),
        ],
        out_specs=(
            pl.BlockSpec((TM_IN, N_HEADS * LANES), row_map),
            pl.BlockSpec((None, ATTN_W, TM_IN), lambda b, i: (b, 0, i)),
            pl.BlockSpec((None, ATTN_W, TM_IN), lambda b, i: (b, 0, i)),
            pl.BlockSpec((None, TM_IN // BK, ATTN_W, BK), lambda b, i: (b, i, 0, 0)),
            pl.BlockSpec((None, TM_IN // BK, ATTN_W, BK), lambda b, i: (b, i, 0, 0)),
            pl.BlockSpec((TM_IN, CONV_W), row_map),
            pl.BlockSpec((None, HALO, CONV_W), lambda b, i: (b, 0, 0)),
        ),
        out_shape=out_shape,
        scratch_shapes=[pltpu.VMEM((HALO + TM_IN, CONV_W), F32)],
        compiler_params=_cparams(("arbitrary", "arbitrary")),
        name="inproj_prompt",
    )(x, g, w_in, bd, qnw, knw, cw)


def _inproj_sample_kernel(x_ref, g_ref, w_ref, bd_ref, qnw_ref, knw_ref, cw_ref, h1_ref, h2_ref,
                          q_ref, k_ref, v_ref, co_ref, u_ref, ubuf, *, seq):
    rows = x_ref.shape[0]
    qn, kn, v, gate_b, u = _inproj_core(x_ref[...], g_ref[...], w_ref, bd_ref[...], qnw_ref[...], knw_ref[...])
    q_ref[...] = qn
    k_ref[...] = kn
    v_ref[...] = v
    u_ref[...] = u
    ubuf[0:HALO, :] = jnp.zeros((HALO, CONV_W), F32)
    ubuf[HALO:HALO + rows, :] = u
    t = lax.broadcasted_iota(I32, (rows, 1), 0) % seq
    x1 = jnp.where(t >= 1, ubuf[HALO - 1:HALO - 1 + rows, :], h1_ref[...])
    x2 = jnp.where(t >= 2, ubuf[HALO - 2:HALO - 2 + rows, :], h2_ref[...])
    conv = cw_ref[0:1, :] * x2 + cw_ref[1:2, :] * x1 + cw_ref[2:3, :] * u
    co_ref[...] = (gate_b * conv).astype(BF16)


def _inproj_sample(x, g, w_in, bd, qnw, knw, cw, halo1, halo2, seq):
    rows = x.shape[0]
    full = lambda a: pl.BlockSpec(a.shape, lambda i: (0,) * a.ndim)
    args = (x, g, w_in, bd, qnw, knw, cw, halo1, halo2)
    shp = lambda w, dt: jax.ShapeDtypeStruct((rows, w), dt)
    out_shape = (shp(ATTN_W, F32), shp(ATTN_W, F32), shp(ATTN_W, F32), shp(CONV_W, BF16), shp(CONV_W, F32))
    return pl.pallas_call(
        functools.partial(_inproj_sample_kernel, seq=seq),
        grid=(1,),
        in_specs=[full(a) for a in args],
        out_specs=tuple(pl.BlockSpec(s.shape, lambda i: (0, 0)) for s in out_shape),
        out_shape=out_shape,
        scratch_shapes=[pltpu.VMEM((HALO + rows, CONV_W), F32)],
        compiler_params=_cparams(("arbitrary",)),
        name="inproj_sample",
    )(*args)


def _attn_prompt_kernel(bias_ref, q_ref, kt_ref, vt_ref, u_ref, o_ref):
    p = pl.program_id(1)
    i = pl.program_id(2)
    tq = q_ref.shape[0]
    kq = tq // BK
    row = lax.broadcasted_iota(I32, (2 * tq, 1), 0)
    second = row >= tq
    bias = jnp.where(second, bias_ref[2 * p + 1], bias_ref[2 * p])
    qry = jnp.where(second, row - tq, row)
    key = lax.broadcasted_iota(I32, (2 * tq, BK), 1)
    feat = lax.broadcasted_iota(I32, (LANES, 2 * BK), 0)
    half = lax.broadcasted_iota(I32, (LANES, 2 * BK), 1) >= BK
    own = (feat >= HEAD_DIM) == half
    u = u_ref[...]
    lane = lax.broadcasted_iota(I32, (1, LANES), 1)
    b_hi = bias.astype(BF16).astype(F32)
    b_cols = jnp.where(lane == 0, b_hi, jnp.where(lane == 1, bias - b_hi, 0.0)).astype(BF16)
    q2 = jnp.concatenate([q_ref[:, 0:LANES], q_ref[:, LANES:2 * LANES]], axis=0)
    q2 = jnp.concatenate([q2, b_cols], axis=1)
    ones_rows = (lax.broadcasted_iota(I32, (LANES, BK), 0) < 2).astype(BF16)

    def tile(j, carry, acc, causal=None):
        kt = jnp.concatenate([kt_ref[j], ones_rows], axis=0)
        vt = vt_ref[j]
        z = jnp.dot(q2, kt, preferred_element_type=F32)
        sp = _softplus2(z)
        if causal is not None:
            sp = jnp.where(causal, sp, 0.0)
        later = jnp.dot(sp.astype(BF16), u, preferred_element_type=F32)
        w = jnp.exp2(z - sp - later + carry)
        if causal is not None:
            w = jnp.where(causal, w, 0.0)
        wb = w.astype(BF16)
        w2 = jnp.concatenate([wb[:tq], wb[tq:]], axis=1)
        v2 = jnp.concatenate([vt, vt], axis=1)
        v2 = jnp.where(own, v2, jnp.zeros_like(v2))
        acc = acc + lax.dot_general(w2, v2, (((1,), (1,)), ((), ())), preferred_element_type=F32)
        carry = carry - jnp.sum(sp, axis=1, keepdims=True)
        return carry, acc

    state = (jnp.zeros((2 * tq, 1), F32), jnp.zeros((tq, LANES), F32))
    for d in reversed(range(kq)):
        state = tile(kq * i + d, state[0], state[1], key + d * BK < qry)

    def run(j_top, n, c):
        for k in range(n):
            c = tile(j_top - k, c[0], c[1])
        return c

    n_below = kq * i
    rem = n_below % ATT_UNROLL
    state = lax.cond(rem != 0, lambda c: run(n_below - 1, kq, c), lambda c: c, state)
    top = n_below - rem
    carry, acc = lax.fori_loop(0, n_below // ATT_UNROLL,
                               lambda it, c: run(top - 1 - ATT_UNROLL * it, ATT_UNROLL, c), state)
    o_ref[...] = acc.astype(BF16)


def _attn_prompt(bias2, q_wide, ktb, vtb, u, batch, seq):
    assert ATT_UNROLL == 2 * (TQ_ATT // BK), "the remainder of the unrolled key loop is one query block's tiles"
    nq = seq // TQ_ATT
    kv_spec = pl.BlockSpec((None, seq // BK, LANES, BK), lambda b, p, i: (b, 0, p, 0))
    return pl.pallas_call(
        _attn_prompt_kernel,
        grid=(batch, N_HEADS // 2, nq),
        in_specs=[
            pl.BlockSpec(memory_space=pltpu.SMEM),
            pl.BlockSpec((TQ_ATT, 2 * LANES), lambda b, p, i: (b * nq + i, p)),
            kv_spec,
            kv_spec,
            pl.BlockSpec((BK, BK), lambda b, p, i: (0, 0)),
        ],
        out_specs=pl.BlockSpec((TQ_ATT, LANES), lambda b, p, i: (b * nq + i, p)),
        out_shape=jax.ShapeDtypeStruct((batch * seq, ATTN_W), BF16),
        compiler_params=_cparams(("arbitrary", "arbitrary", "arbitrary")),
        name="attn_prompt",
    )(bias2, q_wide, ktb, vtb, u)


def _attn_sample_kernel(pt_ref, bias_ref, q_ref, kn_ref, vn_ref, *rest, seq):
    del pt_ref
    npg = PAGES_PER_STEP
    k_refs = rest[:npg]
    v_refs = rest[npg:2 * npg]
    ucat_ref, bm_ref, o_ref, qbd, acc, carry = rest[2 * npg:]
    j = pl.program_id(1)
    rows = N_HEADS * seq

    def page(ktp, vtp, masked):
        z = jnp.dot(qbd[...], ktp, preferred_element_type=F32) + bias_ref[...]
        sp = _softplus2(z)
        if masked:
            t = lax.broadcasted_iota(I32, (rows, PAGE), 0) % seq
            s = lax.broadcasted_iota(I32, (rows, PAGE), 1)
            m = s < t
            sp = jnp.where(m, sp, 0.0)
        lt = jnp.dot(sp.astype(BF16), ucat_ref[...], preferred_element_type=F32)
        w = jnp.exp2(z - sp - lt[:, :PAGE] + carry[...])
        if masked:
            w = jnp.where(m, w, 0.0)
        acc[...] += lax.dot_general(w.astype(BF16), vtp, (((1,), (1,)), ((), ())), preferred_element_type=F32)
        carry[...] -= lt[:, PAGE:]

    @pl.when(j == 0)
    def _():
        q8 = jnp.concatenate([q_ref[...]] * N_HEADS, axis=0)
        qbd[...] = (q8 * bm_ref[...]).astype(BF16)
        acc[...] = jnp.zeros_like(acc)
        carry[...] = jnp.zeros_like(carry)
        page(kn_ref[...], vn_ref[...], True)

    ktc = jnp.concatenate([r[...].reshape(ATTN_W, PAGE).astype(BF16) for r in k_refs], axis=1)
    vtc = jnp.concatenate([r[...].reshape(ATTN_W, PAGE).astype(BF16) for r in v_refs], axis=1)
    zc = jnp.dot(qbd[...], ktc, preferred_element_type=F32)
    run = carry[...]
    ws = []
    for i in range(npg):
        z = zc[:, i * PAGE:(i + 1) * PAGE] + bias_ref[...]
        sp = _softplus2(z)
        lt = jnp.dot(sp.astype(BF16), ucat_ref[...], preferred_element_type=F32)
        ws.append(jnp.exp2(z - sp - lt[:, :PAGE] + run).astype(BF16))
        run = run - lt[:, PAGE:]
    carry[...] = run
    acc[...] += lax.dot_general(jnp.concatenate(ws, axis=1), vtc, (((1,), (1,)), ((), ())),
                                preferred_element_type=F32)

    @pl.when(j == pl.num_programs(1) - 1)
    def _():
        d = acc[...] * bm_ref[...]
        out = d[0:seq, :]
        for h in range(1, N_HEADS):
            out = out + d[h * seq:(h + 1) * seq, :]
        o_ref[...] = out


def _attn_sample(page_table, bias_rows, q_s, kn_pad, vn_pad, cache_k, cache_v, ucat, bm, seq):
    dec_b, n_pages = page_table.shape
    npg = PAGES_PER_STEP
    steps = n_pages // npg

    def page_spec(i):
        return pl.BlockSpec((None, N_HEADS, HEAD_DIM, PAGE),
                            lambda b, j, pt, i=i: (pt[b, n_pages - 1 - (j * npg + i)], 0, 0, 0))

    const2 = lambda b, j, pt: (0, 0)
    per_seq = lambda b, j, pt: (b, 0, 0)
    in_specs = [
        pl.BlockSpec(bias_rows.shape, const2),
        pl.BlockSpec((None, seq, ATTN_W), per_seq),
        pl.BlockSpec((None, ATTN_W, PAGE), per_seq),
        pl.BlockSpec((None, ATTN_W, PAGE), per_seq),
    ] + [page_spec(i) for i in range(npg)] + [page_spec(i) for i in range(npg)] + [
        pl.BlockSpec(ucat.shape, const2),
        pl.BlockSpec(bm.shape, const2),
    ]
    return pl.pallas_call(
        functools.partial(_attn_sample_kernel, seq=seq),
        grid_spec=pltpu.PrefetchScalarGridSpec(
            num_scalar_prefetch=1,
            grid=(dec_b, steps),
            in_specs=in_specs,
            out_specs=pl.BlockSpec((None, seq, ATTN_W), per_seq),
            scratch_shapes=[pltpu.VMEM((N_HEADS * seq, ATTN_W), BF16),
                            pltpu.VMEM((N_HEADS * seq, ATTN_W), F32),
                            pltpu.VMEM((N_HEADS * seq, PAGE), F32)],
        ),
        out_shape=jax.ShapeDtypeStruct((dec_b, seq, ATTN_W), F32),
        compiler_params=_cparams(("arbitrary", "arbitrary")),
        name="attn_sample",
    )(page_table, bias_rows, q_s, kn_pad, vn_pad, *([cache_k] * npg), *([cache_v] * npg), ucat, bm)


def _mixout_kernel(hp_ref, atp_ref, cop_ref, hs_ref, ats_ref, cos_ref, woa_ref, woc_ref, g_ref, wr1_ref, wr2_ref,
                   br_ref, ltri_ref, h1_ref, xn_ref, ti_ref, tw_ref, rk_ref, cnt_ref, cnt_sc, *, np_blk):
    tm = hp_ref.shape[0]
    i = pl.program_id(0)

    @pl.when(i == 0)
    def _():
        cnt_sc[...] = jnp.zeros_like(cnt_sc)

    is_s = i >= np_blk
    h = jnp.where(is_s, hs_ref[...], hp_ref[...])
    at = jnp.where(is_s, ats_ref[...], atp_ref[...])
    co = jnp.where(is_s, cos_ref[...], cop_ref[...])
    h1 = (h + jnp.dot(at, woa_ref[...], preferred_element_type=F32)
          + jnp.dot(co, woc_ref[...], preferred_element_type=F32))
    h1_ref[...] = h1
    xn = _rms(h1, g_ref[...])
    _store_rows(xn_ref, xn)

    xh = xn.astype(BF16)
    xl = (xn - xh.astype(F32)).astype(BF16)
    t1 = jnp.dot(xh, wr1_ref[...], preferred_element_type=F32)
    t2 = jnp.dot(xl, wr2_ref[...], preferred_element_type=F32)
    logits = t1[:, :LANES] + t1[:, LANES:] + t2 + br_ref[...]

    lane_i = lax.broadcasted_iota(I32, (tm, LANES), 1)
    lane = lane_i.astype(F32)
    cur = logits
    vals, idxs, hots = [], [], []
    for _ in range(TOP_K):
        m = jnp.max(cur, axis=-1, keepdims=True)
        idx = jnp.min(jnp.where(cur == m, lane, float(LANES)), axis=-1, keepdims=True)
        hot = lane == idx
        cur = jnp.where(hot, -jnp.inf, cur)
        vals.append(m)
        idxs.append(idx)
        hots.append(hot)
    es = [jnp.exp(v - vals[0]) for v in vals]
    denom = es[0] + es[1] + es[2] + es[3]

    member = hots[0] | hots[1] | hots[2] | hots[3]
    mf = member.astype(F32)
    before = jnp.dot(ltri_ref[...], mf.astype(BF16), preferred_element_type=F32) + cnt_sc[...]
    cnt_sc[...] += jnp.sum(mf, axis=0, keepdims=True)
    cnt_ref[...] = cnt_sc[...]

    ti = jnp.zeros((tm, LANES), I32)
    tw = jnp.zeros((tm, LANES), F32)
    rk = jnp.zeros((tm, LANES), I32)
    for c in range(TOP_K):
        sel = lane_i == c
        rank = jnp.sum(jnp.where(hots[c], before, 0.0), axis=-1, keepdims=True)
        ti = jnp.where(sel, idxs[c].astype(I32), ti)
        tw = jnp.where(sel, es[c] / denom, tw)
        rk = jnp.where(sel, rank.astype(I32), rk)
    ti_ref[...] = ti
    tw_ref[...] = tw
    rk_ref[...] = rk


def _mixout(hp, atp, cop, hs, ats, cos, woa, woc, g, wr1, wr2, br):
    tm = TM_MIX
    np_blk = hp.shape[0] // tm
    ns_blk = hs.shape[0] // tm
    n_all = hp.shape[0] + hs.shape[0]
    ltri = jnp.tril(jnp.ones((tm, tm), F32), -1).astype(BF16)
    const = lambda i: (0, 0)
    row_p = lambda i: (jnp.minimum(i, np_blk - 1), 0)
    row_s = lambda i: (jnp.maximum(i - np_blk, 0), 0)
    row_out = lambda i: (i, 0)
    ins = [hp, atp, cop, hs, ats, cos, woa, woc, g, wr1, wr2, br, ltri]
    in_specs = [
        pl.BlockSpec((tm, D_MODEL), row_p),
        pl.BlockSpec((tm, ATTN_W), row_p),
        pl.BlockSpec((tm, CONV_W), row_p),
        pl.BlockSpec((tm, D_MODEL), row_s),
        pl.BlockSpec((tm, ATTN_W), row_s),
        pl.BlockSpec((tm, CONV_W), row_s),
        pl.BlockSpec(woa.shape, const),
        pl.BlockSpec(woc.shape, const),
        pl.BlockSpec(g.shape, const),
        pl.BlockSpec(wr1.shape, const),
        pl.BlockSpec(wr2.shape, const),
        pl.BlockSpec(br.shape, const),
        pl.BlockSpec(ltri.shape, const),
    ]
    out_shape = (
        jax.ShapeDtypeStruct((n_all, D_MODEL), F32),
        jax.ShapeDtypeStruct((n_all * ROW_SUB, LANES), F32),
        jax.ShapeDtypeStruct((n_all, LANES), I32),
        jax.ShapeDtypeStruct((n_all, LANES), F32),
        jax.ShapeDtypeStruct((n_all, LANES), I32),
    )
    outs = pl.pallas_call(
        functools.partial(_mixout_kernel, np_blk=np_blk),
        grid=(np_blk + ns_blk,),
        in_specs=in_specs,
        out_specs=(
            pl.BlockSpec((tm, D_MODEL), row_out),
            pl.BlockSpec((tm * ROW_SUB, LANES), row_out),
            pl.BlockSpec((tm, LANES), row_out),
            pl.BlockSpec((tm, LANES), row_out),
            pl.BlockSpec((tm, LANES), row_out),
            pl.BlockSpec((1, LANES), const),
        ),
        out_shape=out_shape + (jax.ShapeDtypeStruct((1, LANES), F32),),
        scratch_shapes=[pltpu.VMEM((1, LANES), F32)],
        compiler_params=_cparams(("arbitrary",)),
        name="mixout",
    )(*ins)
    return outs[:5], outs[5]


def _dispatch_kernel(pad_lo_ref, pad_n_ref, dst_ref, x_ref, xs_hbm, zero, sem, zsem):
    t = pl.program_id(0)
    tm = dst_ref.shape[2] // TOP_K

    def rows(ref, i):
        return ref.at[pl.ds(pl.multiple_of(i * ROW_SUB, ROW_SUB), ROW_SUB)]

    def zero_fill(e, wait):
        lo = pad_lo_ref[e]
        n_big = lax.shift_right_logical(pad_n_ref[e], ZERO_SLOTS.bit_length() - 1)
        n_one = pad_n_ref[e] & (ZERO_SLOTS - 1)

        def big(k, carry):
            at = 0 if wait else pl.multiple_of((lo + k * ZERO_SLOTS) * ROW_SUB, ROW_SUB)
            cp = pltpu.make_async_copy(zero, xs_hbm.at[pl.ds(at, ZERO_SLOTS * ROW_SUB)], zsem.at[0])
            cp.wait() if wait else cp.start()
            return carry

        def one(k, carry):
            cp = pltpu.make_async_copy(rows(zero, 0), rows(xs_hbm, 0 if wait else lo + n_big * ZERO_SLOTS + k),
                                       zsem.at[0])
            cp.wait() if wait else cp.start()
            return carry

        lax.fori_loop(0, n_big, big, 0)
        lax.fori_loop(0, n_one, one, 0)

    @pl.when(t == 0)
    def _():
        zero[...] = jnp.zeros_like(zero)
        for e in range(pad_n_ref.shape[0]):
            zero_fill(e, wait=False)

    def issue(r, carry):
        for c in range(TOP_K):
            pltpu.make_async_copy(rows(x_ref, r), rows(xs_hbm, dst_ref[0, 0, r * TOP_K + c]),
                                  sem.at[0]).start(priority=c % 2)
        return carry
    lax.fori_loop(0, tm, issue, 0, unroll=4)

    def drain(r, carry):
        for c in range(TOP_K):
            pltpu.make_async_copy(rows(x_ref, 0), rows(xs_hbm, 0), sem.at[0]).wait()
        return carry
    lax.fori_loop(0, tm, drain, 0, unroll=4)

    @pl.when(t == pl.num_programs(0) - 1)
    def _():
        for e in range(pad_n_ref.shape[0]):
            zero_fill(e, wait=True)


def _dispatch(pad_lo, pad_n, dest_blk, xn_all, n_slots):
    nblk = dest_blk.shape[0]
    return pl.pallas_call(
        _dispatch_kernel,
        grid_spec=pltpu.PrefetchScalarGridSpec(
            num_scalar_prefetch=2,
            grid=(nblk,),
            in_specs=[
                pl.BlockSpec((1, 1, dest_blk.shape[2]), lambda t, lo, n: (t, 0, 0), memory_space=pltpu.SMEM),
                pl.BlockSpec((dest_blk.shape[2] // TOP_K * ROW_SUB, LANES), lambda t, lo, n: (t, 0)),
            ],
            out_specs=pl.BlockSpec(memory_space=pl.ANY),
            scratch_shapes=[pltpu.VMEM((ZERO_SLOTS * ROW_SUB, LANES), F32),
                            pltpu.SemaphoreType.DMA((1,)),
                            pltpu.SemaphoreType.DMA((1,))],
        ),
        out_shape=jax.ShapeDtypeStruct((n_slots * ROW_SUB, LANES), F32),
        compiler_params=_cparams(("arbitrary",)),
        name="dispatch",
    )(pad_lo, pad_n, dest_blk, xn_all)


def _moe_kernel(te_ref, nu_ref, x_ref, wgu_ref, bgu_ref, wd_ref, bd_ref, y_ref, wgu_bf, wd_bf):
    t = pl.program_id(0)
    n_used = nu_ref[0]

    @pl.when(t >= n_used)
    def _():
        y_ref[...] = jnp.zeros_like(y_ref)

    @pl.when(t < n_used)
    def _():
        prev_e = te_ref[jnp.maximum(t - 1, 0)]

        @pl.when((t == 0) | (te_ref[t] != prev_e))
        def _():
            wgu_bf[...] = wgu_ref[...].astype(BF16)
            wd_bf[...] = wd_ref[...].astype(BF16)

        x = _load_rows(x_ref).astype(BF16)
        hh = jnp.dot(x, wgu_bf[...], preferred_element_type=F32) + bgu_ref[...]
        g = jnp.minimum(hh[:, :D_FF], SWIGLU_LIMIT)
        u = jnp.clip(hh[:, D_FF:], -SWIGLU_LIMIT, SWIGLU_LIMIT)
        act = (u + 1.0) * g * jax.nn.sigmoid(SWIGLU_ALPHA * g)
        _store_rows(y_ref, jnp.dot(act.astype(BF16), wd_bf[...], preferred_element_type=F32) + bd_ref[...])


def _moe(tile_expert, n_used, x_sorted, wgu, bgu, wd, bd):
    tm = TM_MOE
    n_tiles = x_sorted.shape[0] // (tm * ROW_SUB)
    e_map = lambda t, te, nu: (te[t], 0, 0)
    return pl.pallas_call(
        _moe_kernel,
        grid_spec=pltpu.PrefetchScalarGridSpec(
            num_scalar_prefetch=2,
            grid=(n_tiles,),
            in_specs=[
                pl.BlockSpec((tm * ROW_SUB, LANES), lambda t, te, nu: (jnp.minimum(t, nu[0] - 1), 0)),
                pl.BlockSpec((None, D_MODEL, 2 * D_FF), e_map),
                pl.BlockSpec((None, 1, 2 * D_FF), e_map),
                pl.BlockSpec((None, D_FF, D_MODEL), e_map),
                pl.BlockSpec((None, 1, D_MODEL), e_map),
            ],
            out_specs=pl.BlockSpec((tm * ROW_SUB, LANES), lambda t, te, nu: (t, 0)),
            scratch_shapes=[pltpu.VMEM((D_MODEL, 2 * D_FF), BF16),
                            pltpu.VMEM((D_FF, D_MODEL), BF16)],
        ),
        out_shape=jax.ShapeDtypeStruct(x_sorted.shape, F32),
        compiler_params=_cparams(("arbitrary",)),
        name="moe",
    )(tile_expert, n_used, x_sorted, wgu, bgu, wd, bd)


def _ple_kernel(dst_cur, dst_nxt, y_hbm, h1_ref, tw_ref, p_ref, g_ref, wg_ref, wp_ref, o_ref, ybuf, sem):
    t = pl.program_id(0)
    nt = pl.num_programs(0)
    slot = t % 2
    tm = o_ref.shape[0]

    def row_copy(row, c, r, s):
        return pltpu.make_async_copy(y_hbm.at[pl.ds(pl.multiple_of(row * ROW_SUB, ROW_SUB), ROW_SUB)],
                                     ybuf.at[s, c, pl.ds(pl.multiple_of(r * ROW_SUB, ROW_SUB), ROW_SUB)], sem.at[s])

    def gather(dst, s):
        def body(r, carry):
            for c in range(TOP_K):
                row_copy(dst[0, 0, r * TOP_K + c], c, r, s).start(priority=c % 2)
            return carry
        lax.fori_loop(0, tm, body, 0, unroll=4)

    def drain(s):
        def body(r, carry):
            for c in range(TOP_K):
                row_copy(0, c, r, s).wait()
            return carry
        lax.fori_loop(0, tm, body, 0, unroll=4)

    @pl.when(t == 0)
    def _():
        gather(dst_cur, 0)

    @pl.when(t + 1 < nt)
    def _():
        gather(dst_nxt, 1 - slot)

    drain(slot)
    tw = tw_ref[...]
    moe = tw[:, 0:1] * _load_rows(ybuf.at[slot, 0])
    for c in range(1, TOP_K):
        moe = moe + tw[:, c:c + 1] * _load_rows(ybuf.at[slot, c])
    h2 = h1_ref[...] + moe
    gate = jax.nn.sigmoid(jnp.dot(_rms(h2, g_ref[...]).astype(BF16), wg_ref[...], preferred_element_type=F32))
    proj = jnp.dot(p_ref[...].astype(BF16), wp_ref[...], preferred_element_type=F32)
    o_ref[...] = h2 + gate * proj


def _ple(dest, y_sorted, h1_all, tw_all, p, g, wg, wp, row_off):
    rows = p.shape[0]
    tm = min(TM_PLE, rows)
    nblk = rows // tm
    blk_off = row_off // tm
    last = nblk - 1
    const = lambda t: (0, 0)
    return pl.pallas_call(
        _ple_kernel,
        grid=(nblk,),
        in_specs=[
            pl.BlockSpec((1, 1, tm * TOP_K), lambda t: (t + blk_off, 0, 0), memory_space=pltpu.SMEM),
            pl.BlockSpec((1, 1, tm * TOP_K), lambda t: (jnp.minimum(t + 1, last) + blk_off, 0, 0),
                         memory_space=pltpu.SMEM),
            pl.BlockSpec(memory_space=pl.ANY),
            pl.BlockSpec((tm, D_MODEL), lambda t: (t + blk_off, 0)),
            pl.BlockSpec((tm, LANES), lambda t: (t + blk_off, 0)),
            pl.BlockSpec((tm, p.shape[1]), lambda t: (t, 0)),
            pl.BlockSpec(g.shape, const),
            pl.BlockSpec(wg.shape, const),
            pl.BlockSpec(wp.shape, const),
        ],
        out_specs=pl.BlockSpec((tm, D_MODEL), lambda t: (t, 0)),
        out_shape=jax.ShapeDtypeStruct((rows, D_MODEL), F32),
        scratch_shapes=[pltpu.VMEM((2, TOP_K, tm * ROW_SUB, LANES), F32), pltpu.SemaphoreType.DMA((2,))],
        compiler_params=_cparams(("arbitrary",)),
        name="ple_%d" % rows,
    )(dest, dest, y_sorted, h1_all, tw_all, p, g, wg, wp)


def kernel(x_prompt, x_sample, p_prompt, p_sample, cache_k, cache_v, state_conv, page_table, g_mix, w_in,
           q_norm_w, k_norm_w, sb_bias, conv_w, w_out, g_moe, w_router, b_router, w_gate_up, b_gate_up, w_down,
           b_down, g_ple, w_ple_gate, w_ple_proj):
    depth = g_mix.shape[0]
    assert depth == 1, "single-layer step"
    batch, seq, _ = x_prompt.shape
    dec_b, dec_seq, _ = x_sample.shape
    n_pool = cache_k.shape[1]
    n_p = batch * seq
    n_s = dec_b * dec_seq
    n_all = n_p + n_s
    assert seq % TM_IN == 0 and TM_IN % BK == 0 and n_p % TM_MIX == 0 and n_s % TM_MIX == 0
    assert seq % TQ_ATT == 0 and TQ_ATT % BK == 0
    assert n_p % TM_PLE == 0 and n_s % TM_PLE == 0
    assert page_table.shape[1] % PAGES_PER_STEP == 0 and cache_k.shape[2] == PAGE

    log2e = math.log2(math.e)
    scale2 = log2e / math.sqrt(HEAD_DIM)
    bias2 = sb_bias[0] * log2e
    w_in_bf = w_in[0].astype(BF16)
    head_id = jnp.arange(ATTN_W) // HEAD_DIM
    bd = jnp.where(head_id[:, None] == head_id[None, :], 1.0 / HEAD_DIM, 0.0).astype(BF16)
    qnw = (jnp.tile(q_norm_w[0], N_HEADS) * scale2)[None, :]
    knw = jnp.tile(k_norm_w[0], N_HEADS)[None, :]
    g_mix2 = g_mix[0][None, :]
    cw = conv_w[0]
    woa = w_out[0, :ATTN_W].astype(BF16)
    woc = w_out[0, ATTN_W:].astype(BF16)
    wr = jnp.pad(w_router[0], ((0, 0), (0, LANES - N_EXPERTS)))
    wr_hi = wr.astype(BF16)
    wr_lo = (wr - wr_hi.astype(F32)).astype(BF16)
    wr1 = jnp.concatenate([wr_hi, wr_lo], axis=1)
    br = jnp.concatenate([b_router[0], jnp.full((LANES - N_EXPERTS,), -1e30, F32)])[None, :]
    g_moe2 = g_moe[0][None, :]
    g_ple2 = g_ple[0][None, :]
    wpg = w_ple_gate[0].astype(BF16)
    wpp = w_ple_proj[0].astype(BF16)

    xp = x_prompt.reshape(n_p, D_MODEL)
    xs = x_sample.reshape(n_s, D_MODEL)
    qw_p, kt_p, vt_p, ktb_p, vtb_p, co_p, ut_p = _inproj_prompt(xp, g_mix2, w_in_bf, bd, qnw, knw, cw, batch, seq)
    st = state_conv[0]
    zero_s = jnp.zeros((dec_b, dec_seq, CONV_W), F32)
    halo1 = zero_s.at[:, 0].set(st[:, 1]).reshape(n_s, CONV_W)
    halo2 = zero_s.at[:, 0].set(st[:, 0]).at[:, 1].set(st[:, 1]).reshape(n_s, CONV_W)
    q_s, k_s, v_s, co_s, u_s = _inproj_sample(xs, g_mix2, w_in_bf, bd, qnw, knw, cw, halo1, halo2, dec_seq)

    u_tri = jnp.tril(jnp.ones((BK, BK), F32), -1).astype(BF16)
    attn_p = _attn_prompt(bias2, qw_p, ktb_p, vtb_p, u_tri, batch, seq)

    def new_page(a):
        at = jnp.swapaxes(a.reshape(dec_b, dec_seq, ATTN_W), 1, 2).astype(BF16)
        return jnp.pad(at, ((0, 0), (0, 0), (0, PAGE - dec_seq)))

    row_head = jnp.arange(N_HEADS * dec_seq) // dec_seq
    bias_rows = jnp.broadcast_to(bias2[row_head][:, None], (N_HEADS * dec_seq, PAGE)).astype(F32)
    bm = (row_head[:, None] == head_id[None, :]).astype(F32)
    ucat = jnp.concatenate([jnp.tril(jnp.ones((PAGE, PAGE), F32), -1), jnp.ones((PAGE, PAGE), F32)],
                           axis=1).astype(BF16)
    ck = jnp.transpose(cache_k[0], (0, 2, 3, 1))
    cv = jnp.transpose(cache_v[0], (0, 2, 3, 1))
    attn_s = _attn_sample(page_table, bias_rows, q_s.reshape(dec_b, dec_seq, ATTN_W), new_page(k_s), new_page(v_s),
                          ck, cv, ucat, bm, dec_seq)

    bufs, cnt = _mixout(xp, attn_p, co_p, xs, attn_s.reshape(n_s, ATTN_W).astype(BF16), co_s, woa, woc, g_moe2,
                        wr1, wr_hi, br)
    h1_all, xn_all, ti_all, tw_all, rk_all = bufs

    n_tiles = (n_all * TOP_K + N_EXPERTS * (TM_MOE - 1)) // TM_MOE + 1
    counts = cnt[0, :N_EXPERTS].astype(I32)
    padded = ((counts + TM_MOE - 1) // TM_MOE) * TM_MOE
    ends = jnp.cumsum(padded)
    starts = ends - padded
    top_i = ti_all[:, :TOP_K]
    dest = starts[top_i] + rk_all[:, :TOP_K]
    tile_start = jnp.arange(n_tiles, dtype=I32) * TM_MOE
    tile_expert = jnp.minimum(jnp.sum((tile_start[:, None] >= ends[None, :]).astype(I32), axis=1), N_EXPERTS - 1)
    n_used = (ends[-1] // TM_MOE).astype(I32)[None]
    dest_blk = dest.reshape(n_all // TM_PLE, 1, TM_PLE * TOP_K)

    n_slots = n_tiles * TM_MOE
    pad_lo = jnp.concatenate([starts + counts, ends[-1:]])
    pad_n = jnp.concatenate([padded - counts, n_slots - ends[-1:]])
    x_sorted = _dispatch(pad_lo, pad_n, dest_blk, xn_all, n_slots)
    y_sorted = _moe(tile_expert, n_used, x_sorted, w_gate_up[0], b_gate_up[0][:, None, :], w_down[0],
                    b_down[0][:, None, :])

    y_p = _ple(dest_blk, y_sorted, h1_all, tw_all, p_prompt[0].reshape(n_p, -1), g_ple2, wpg, wpp, 0)
    y_s = _ple(dest_blk, y_sorted, h1_all, tw_all, p_sample[0].reshape(n_s, -1), g_ple2, wpg, wpp, n_p)

    new_conv_p = ut_p[:, HALO - 2:, :]
    new_conv_s = u_s.reshape(dec_b, dec_seq, CONV_W)[:, dec_seq - 2:, :]
    return (
        y_p.reshape(batch, seq, D_MODEL),
        y_s.reshape(dec_b, dec_seq, D_MODEL),
        jnp.transpose(kt_p.reshape(batch, N_HEADS, HEAD_DIM, seq), (0, 3, 1, 2))[None],
        jnp.transpose(vt_p.reshape(batch, N_HEADS, HEAD_DIM, seq), (0, 3, 1, 2))[None],
        new_conv_p[None],
        k_s.reshape(1, dec_b, dec_seq, N_HEADS, HEAD_DIM),
        v_s.reshape(1, dec_b, dec_seq, N_HEADS, HEAD_DIM),
        new_conv_s[None],
    )
```

```python
import functools
import math

import jax
import jax.numpy as jnp
from jax import lax
from jax.experimental import pallas as pl
from jax.experimental.pallas import tpu as pltpu

F32 = jnp.float32
BF16 = jnp.bfloat16
I32 = jnp.int32

D_MODEL = 1024
N_HEADS = 8
HEAD_DIM = 64
ATTN_W = N_HEADS * HEAD_DIM
CONV_W = 512
N_EXPERTS = 32
TOP_K = 4
D_FF = 1024
RMS_EPS = 1e-6
SWIGLU_LIMIT = 7.0
SWIGLU_ALPHA = 1.702
PAGE = 128

LANES = 128
HALO = 8
VMEM_LIMIT = 56 * 1024 * 1024

TM_IN = 512
BK = 256
TQ_ATT = 512
ATT_UNROLL = 4
PAGES_PER_STEP = 16
TM_MIX = 256
TM_MOE = 512
ZERO_SLOTS = 16
TM_PLE = 256


def _cparams(sem):
    return pltpu.CompilerParams(dimension_semantics=sem, vmem_limit_bytes=VMEM_LIMIT)


ROW_SUB = D_MODEL // LANES


def _store_rows(ref, x):
    rows = x.shape[0]
    for c in range(ROW_SUB):
        ref[pl.ds(c, rows, stride=ROW_SUB), :] = x[:, c * LANES:(c + 1) * LANES]


def _load_rows(ref):
    rows = ref.shape[0] // ROW_SUB
    return jnp.concatenate([ref[pl.ds(c, rows, stride=ROW_SUB), :] for c in range(ROW_SUB)], axis=1)


def _rms(x, g):
    ms = jnp.mean(x * x, axis=-1, keepdims=True)
    return x * lax.rsqrt(ms + RMS_EPS) * g


def _softplus2(z2):
    return jnp.maximum(z2, 0.0) + jnp.log2(1.0 + jnp.exp2(jnp.minimum(z2, -z2)))


def _inproj_core(x, g, w_ref, bd, qnw, knw):
    a = _rms(x, g).astype(BF16)

    def sec(i):
        return jnp.dot(a, w_ref[:, i * 512:(i + 1) * 512], preferred_element_type=F32)

    def head_norm(t, w):
        m = jnp.dot((t * t).astype(BF16), bd, preferred_element_type=F32)
        return t * lax.rsqrt(m + RMS_EPS) * w

    qn = head_norm(sec(0), qnw)
    kn = head_norm(sec(1), knw)
    v = sec(2)
    gate_b = sec(3)
    u = sec(4) * sec(5)
    return qn, kn, v, gate_b, u


def _inproj_prompt_kernel(x_ref, g_ref, w_ref, bd_ref, qnw_ref, knw_ref, cw_ref,
                          qw_ref, kt_ref, vt_ref, ktb_ref, vtb_ref, co_ref, ut_ref, ubuf):
    tm = x_ref.shape[0]

    @pl.when(pl.program_id(1) == 0)
    def _():
        ubuf[0:HALO, :] = jnp.zeros((HALO, CONV_W), F32)

    qn, kn, v, gate_b, u = _inproj_core(x_ref[...], g_ref[...], w_ref, bd_ref[...], qnw_ref[...], knw_ref[...])

    lane = lax.broadcasted_iota(I32, (1, LANES), 1)
    lo = (lane < HEAD_DIM).astype(F32)
    hi = 1.0 - lo
    for h in range(N_HEADS):
        p = h // 2
        blk = qn[:, p * LANES:(p + 1) * LANES] * (lo if h % 2 == 0 else hi)
        qw_ref[:, h * LANES:(h + 1) * LANES] = blk.astype(BF16)

    kt = kn.T
    vt = v.T
    kt_ref[...] = kt
    vt_ref[...] = vt
    for c in range(tm // BK):
        ktb_ref[c] = kt[:, c * BK:(c + 1) * BK].astype(BF16)
        vtb_ref[c] = vt[:, c * BK:(c + 1) * BK].astype(BF16)

    ubuf[HALO:HALO + tm, :] = u
    conv = (cw_ref[0:1, :] * ubuf[HALO - 2:HALO - 2 + tm, :]
            + cw_ref[1:2, :] * ubuf[HALO - 1:HALO - 1 + tm, :]
            + cw_ref[2:3, :] * u)
    co_ref[...] = (gate_b * conv).astype(BF16)
    tail = u[tm - HALO:tm, :]
    ubuf[0:HALO, :] = tail
    ut_ref[...] = tail


def _inproj_prompt(x, g, w_in, bd, qnw, knw, cw, batch, seq):
    rows = batch * seq
    nblk = seq // TM_IN
    row_map = lambda b, i: (b * nblk + i, 0)
    const = lambda b, i: (0, 0)
    out_shape = (
        jax.ShapeDtypeStruct((rows, N_HEADS * LANES), BF16),
        jax.ShapeDtypeStruct((batch, ATTN_W, seq), F32),
        jax.ShapeDtypeStruct((batch, ATTN_W, seq), F32),
        jax.ShapeDtypeStruct((batch, seq // BK, ATTN_W, BK), BF16),
        jax.ShapeDtypeStruct((batch, seq // BK, ATTN_W, BK), BF16),
        jax.ShapeDtypeStruct((rows, CONV_W), BF16),
        jax.ShapeDtypeStruct((batch, HALO, CONV_W), F32),
    )
    return pl.pallas_call(
        _inproj_prompt_kernel,
        grid=(batch, nblk),
        in_specs=[
            pl.BlockSpec((TM_IN, D_MODEL), row_map),
            pl.BlockSpec((1, D_MODEL), const),
            pl.BlockSpec(w_in.shape, const),
            pl.BlockSpec(bd.shape, const),
            pl.BlockSpec((1, ATTN_W), const),
            pl.BlockSpec((1, ATTN_W), const),
            pl.BlockSpec(cw.shape, const),
        ],
        out_specs=(
            pl.BlockSpec((TM_IN, N_HEADS * LANES), row_map),
            pl.BlockSpec((None, ATTN_W, TM_IN), lambda b, i: (b, 0, i)),
            pl.BlockSpec((None, ATTN_W, TM_IN), lambda b, i: (b, 0, i)),
            pl.BlockSpec((None, TM_IN // BK, ATTN_W, BK), lambda b, i: (b, i, 0, 0)),
            pl.BlockSpec((None, TM_IN // BK, ATTN_W, BK), lambda b, i: (b, i, 0, 0)),
            pl.BlockSpec((TM_IN, CONV_W), row_map),
            pl.BlockSpec((None, HALO, CONV_W), lambda b, i: (b, 0, 0)),
        ),
        out_shape=out_shape,
        scratch_shapes=[pltpu.VMEM((HALO + TM_IN, CONV_W), F32)],
        compiler_params=_cparams(("arbitrary", "arbitrary")),
        name="inproj_prompt",
    )(x, g, w_in, bd, qnw, knw, cw)


def _inproj_sample_kernel(x_ref, g_ref, w_ref, bd_ref, qnw_ref, knw_ref, cw_ref, h1_ref, h2_ref,
                          q_ref, k_ref, v_ref, co_ref, u_ref, ubuf, *, seq):
    rows = x_ref.shape[0]
    qn, kn, v, gate_b, u = _inproj_core(x_ref[...], g_ref[...], w_ref, bd_ref[...], qnw_ref[...], knw_ref[...])
    q_ref[...] = qn
    k_ref[...] = kn
    v_ref[...] = v
    u_ref[...] = u
    ubuf[0:HALO, :] = jnp.zeros((HALO, CONV_W), F32)
    ubuf[HALO:HALO + rows, :] = u
    t = lax.broadcasted_iota(I32, (rows, 1), 0) % seq
    x1 = jnp.where(t >= 1, ubuf[HALO - 1:HALO - 1 + rows, :], h1_ref[...])
    x2 = jnp.where(t >= 2, ubuf[HALO - 2:HALO - 2 + rows, :], h2_ref[...])
    conv = cw_ref[0:1, :] * x2 + cw_ref[1:2, :] * x1 + cw_ref[2:3, :] * u
    co_ref[...] = (gate_b * conv).astype(BF16)


def _inproj_sample(x, g, w_in, bd, qnw, knw, cw, halo1, halo2, seq):
    rows = x.shape[0]
    full = lambda a: pl.BlockSpec(a.shape, lambda i: (0,) * a.ndim)
    args = (x, g, w_in, bd, qnw, knw, cw, halo1, halo2)
    shp = lambda w, dt: jax.ShapeDtypeStruct((rows, w), dt)
    out_shape = (shp(ATTN_W, F32), shp(ATTN_W, F32), shp(ATTN_W, F32), shp(CONV_W, BF16), shp(CONV_W, F32))
    return pl.pallas_call(
        functools.partial(_inproj_sample_kernel, seq=seq),
        grid=(1,),
        in_specs=[full(a) for a in args],
        out_specs=tuple(pl.BlockSpec(s.shape, lambda i: (0, 0)) for s in out_shape),
        out_shape=out_shape,
        scratch_shapes=[pltpu.VMEM((HALO + rows, CONV_W), F32)],
        compiler_params=_cparams(("arbitrary",)),
        name="inproj_sample",
    )(*args)


def _attn_prompt_kernel(bias_ref, q_ref, kt_ref, vt_ref, u_ref, o_ref):
    p = pl.program_id(1)
    i = pl.program_id(2)
    tq = q_ref.shape[0]
    kq = tq // BK
    row = lax.broadcasted_iota(I32, (2 * tq, 1), 0)
    second = row >= tq
    bias = jnp.where(second, bias_ref[2 * p + 1], bias_ref[2 * p])
    qry = jnp.where(second, row - tq, row)
    key = lax.broadcasted_iota(I32, (2 * tq, BK), 1)
    feat = lax.broadcasted_iota(I32, (LANES, 2 * BK), 0)
    half = lax.broadcasted_iota(I32, (LANES, 2 * BK), 1) >= BK
    own = (feat >= HEAD_DIM) == half
    u = u_ref[...]
    lane = lax.broadcasted_iota(I32, (1, LANES), 1)
    b_hi = bias.astype(BF16).astype(F32)
    b_cols = jnp.where(lane == 0, b_hi, jnp.where(lane == 1, bias - b_hi, 0.0)).astype(BF16)
    q2 = jnp.concatenate([q_ref[:, 0:LANES], q_ref[:, LANES:2 * LANES]], axis=0)
    q2 = jnp.concatenate([q2, b_cols], axis=1)
    ones_rows = (lax.broadcasted_iota(I32, (LANES, BK), 0) < 2).astype(BF16)

    def tile(j, carry, acc, causal=None):
        kt = jnp.concatenate([kt_ref[j], ones_rows], axis=0)
        vt = vt_ref[j]
        z = jnp.dot(q2, kt, preferred_element_type=F32)
        sp = _softplus2(z)
        if causal is not None:
            sp = jnp.where(causal, sp, 0.0)
        later = jnp.dot(sp.astype(BF16), u, preferred_element_type=F32)
        w = jnp.exp2(z - sp - later + carry)
        if causal is not None:
            w = jnp.where(causal, w, 0.0)
        wb = w.astype(BF16)
        w2 = jnp.concatenate([wb[:tq], wb[tq:]], axis=1)
        v2 = jnp.concatenate([vt, vt], axis=1)
        v2 = jnp.where(own, v2, jnp.zeros_like(v2))
        acc = acc + lax.dot_general(w2, v2, (((1,), (1,)), ((), ())), preferred_element_type=F32)
        carry = carry - jnp.sum(sp, axis=1, keepdims=True)
        return carry, acc

    state = (jnp.zeros((2 * tq, 1), F32), jnp.zeros((tq, LANES), F32))
    for d in reversed(range(kq)):
        state = tile(kq * i + d, state[0], state[1], key + d * BK < qry)

    def run(j_top, n, c):
        for k in range(n):
            c = tile(j_top - k, c[0], c[1])
        return c

    n_below = kq * i
    rem = n_below % ATT_UNROLL
    state = lax.cond(rem != 0, lambda c: run(n_below - 1, kq, c), lambda c: c, state)
    top = n_below - rem
    carry, acc = lax.fori_loop(0, n_below // ATT_UNROLL,
                               lambda it, c: run(top - 1 - ATT_UNROLL * it, ATT_UNROLL, c), state)
    o_ref[...] = acc.astype(BF16)


def _attn_prompt(bias2, q_wide, ktb, vtb, u, batch, seq):
    assert ATT_UNROLL == 2 * (TQ_ATT // BK), "the remainder of the unrolled key loop is one query block's tiles"
    nq = seq // TQ_ATT
    kv_spec = pl.BlockSpec((None, seq // BK, LANES, BK), lambda b, p, i: (b, 0, p, 0))
    return pl.pallas_call(
        _attn_prompt_kernel,
        grid=(batch, N_HEADS // 2, nq),
        in_specs=[
            pl.BlockSpec(memory_space=pltpu.SMEM),
            pl.BlockSpec((TQ_ATT, 2 * LANES), lambda b, p, i: (b * nq + i, p)),
            kv_spec,
            kv_spec,
            pl.BlockSpec((BK, BK), lambda b, p, i: (0, 0)),
        ],
        out_specs=pl.BlockSpec((TQ_ATT, LANES), lambda b, p, i: (b * nq + i, p)),
        out_shape=jax.ShapeDtypeStruct((batch * seq, ATTN_W), BF16),
        compiler_params=_cparams(("arbitrary", "arbitrary", "arbitrary")),
        name="attn_prompt",
    )(bias2, q_wide, ktb, vtb, u)


def _attn_sample_kernel(pt_ref, bias_ref, q_ref, kn_ref, vn_ref, *rest, seq):
    del pt_ref
    npg = PAGES_PER_STEP
    k_refs = rest[:npg]
    v_refs = rest[npg:2 * npg]
    ucat_ref, bm_ref, o_ref, qbd, acc, carry = rest[2 * npg:]
    j = pl.program_id(1)
    rows = N_HEADS * seq

    def page(ktp, vtp, masked):
        z = jnp.dot(qbd[...], ktp, preferred_element_type=F32) + bias_ref[...]
        sp = _softplus2(z)
        if masked:
            t = lax.broadcasted_iota(I32, (rows, PAGE), 0) % seq
            s = lax.broadcasted_iota(I32, (rows, PAGE), 1)
            m = s < t
            sp = jnp.where(m, sp, 0.0)
        lt = jnp.dot(sp.astype(BF16), ucat_ref[...], preferred_element_type=F32)
        w = jnp.exp2(z - sp - lt[:, :PAGE] + carry[...])
        if masked:
            w = jnp.where(m, w, 0.0)
        acc[...] += lax.dot_general(w.astype(BF16), vtp, (((1,), (1,)), ((), ())), preferred_element_type=F32)
        carry[...] -= lt[:, PAGE:]

    @pl.when(j == 0)
    def _():
        q8 = jnp.concatenate([q_ref[...]] * N_HEADS, axis=0)
        qbd[...] = (q8 * bm_ref[...]).astype(BF16)
        acc[...] = jnp.zeros_like(acc)
        carry[...] = jnp.zeros_like(carry)
        page(kn_ref[...], vn_ref[...], True)

    ktc = jnp.concatenate([r[...].reshape(ATTN_W, PAGE).astype(BF16) for r in k_refs], axis=1)
    vtc = jnp.concatenate([r[...].reshape(ATTN_W, PAGE).astype(BF16) for r in v_refs], axis=1)
    zc = jnp.dot(qbd[...], ktc, preferred_element_type=F32)
    run = carry[...]
    ws = []
    for i in range(npg):
        z = zc[:, i * PAGE:(i + 1) * PAGE] + bias_ref[...]
        sp = _softplus2(z)
        lt = jnp.dot(sp.astype(BF16), ucat_ref[...], preferred_element_type=F32)
        ws.append(jnp.exp2(z - sp - lt[:, :PAGE] + run).astype(BF16))
        run = run - lt[:, PAGE:]
    carry[...] = run
    acc[...] += lax.dot_general(jnp.concatenate(ws, axis=1), vtc, (((1,), (1,)), ((), ())),
                                preferred_element_type=F32)

    @pl.when(j == pl.num_programs(1) - 1)
    def _():
        d = acc[...] * bm_ref[...]
        out = d[0:seq, :]
        for h in range(1, N_HEADS):
            out = out + d[h * seq:(h + 1) * seq, :]
        o_ref[...] = out


def _attn_sample(page_table, bias_rows, q_s, kn_pad, vn_pad, cache_k, cache_v, ucat, bm, seq):
    dec_b, n_pages = page_table.shape
    npg = PAGES_PER_STEP
    steps = n_pages // npg

    def page_spec(i):
        return pl.BlockSpec((None, N_HEADS, HEAD_DIM, PAGE),
                            lambda b, j, pt, i=i: (pt[b, n_pages - 1 - (j * npg + i)], 0, 0, 0))

    const2 = lambda b, j, pt: (0, 0)
    per_seq = lambda b, j, pt: (b, 0, 0)
    in_specs = [
        pl.BlockSpec(bias_rows.shape, const2),
        pl.BlockSpec((None, seq, ATTN_W), per_seq),
        pl.BlockSpec((None, ATTN_W, PAGE), per_seq),
        pl.BlockSpec((None, ATTN_W, PAGE), per_seq),
    ] + [page_spec(i) for i in range(npg)] + [page_spec(i) for i in range(npg)] + [
        pl.BlockSpec(ucat.shape, const2),
        pl.BlockSpec(bm.shape, const2),
    ]
    return pl.pallas_call(
        functools.partial(_attn_sample_kernel, seq=seq),
        grid_spec=pltpu.PrefetchScalarGridSpec(
            num_scalar_prefetch=1,
            grid=(dec_b, steps),
            in_specs=in_specs,
            out_specs=pl.BlockSpec((None, seq, ATTN_W), per_seq),
            scratch_shapes=[pltpu.VMEM((N_HEADS * seq, ATTN_W), BF16),
                            pltpu.VMEM((N_HEADS * seq, ATTN_W), F32),
                            pltpu.VMEM((N_HEADS * seq, PAGE), F32)],
        ),
        out_shape=jax.ShapeDtypeStruct((dec_b, seq, ATTN_W), F32),
        compiler_params=_cparams(("arbitrary", "arbitrary")),
        name="attn_sample",
    )(page_table, bias_rows, q_s, kn_pad, vn_pad, *([cache_k] * npg), *([cache_v] * npg), ucat, bm)


def _mixout_kernel(hp_ref, atp_ref, cop_ref, hs_ref, ats_ref, cos_ref, woa_ref, woc_ref, g_ref, wr1_ref, wr2_ref,
                   br_ref, ltri_ref, h1_ref, xn_ref, ti_ref, tw_ref, rk_ref, cnt_ref, cnt_sc, *, np_blk):
    tm = hp_ref.shape[0]
    i = pl.program_id(0)

    @pl.when(i == 0)
    def _():
        cnt_sc[...] = jnp.zeros_like(cnt_sc)

    is_s = i >= np_blk
    h = jnp.where(is_s, hs_ref[...], hp_ref[...])
    at = jnp.where(is_s, ats_ref[...], atp_ref[...])
    co = jnp.where(is_s, cos_ref[...], cop_ref[...])
    h1 = (h + jnp.dot(at, woa_ref[...], preferred_element_type=F32)
          + jnp.dot(co, woc_ref[...], preferred_element_type=F32))
    h1_ref[...] = h1
    xn = _rms(h1, g_ref[...])
    _store_rows(xn_ref, xn)

    xh = xn.astype(BF16)
    xl = (xn - xh.astype(F32)).astype(BF16)
    t1 = jnp.dot(xh, wr1_ref[...], preferred_element_type=F32)
    t2 = jnp.dot(xl, wr2_ref[...], preferred_element_type=F32)
    logits = t1[:, :LANES] + t1[:, LANES:] + t2 + br_ref[...]

    lane_i = lax.broadcasted_iota(I32, (tm, LANES), 1)
    lane = lane_i.astype(F32)
    cur = logits
    vals, idxs, hots = [], [], []
    for _ in range(TOP_K):
        m = jnp.max(cur, axis=-1, keepdims=True)
        idx = jnp.min(jnp.where(cur == m, lane, float(LANES)), axis=-1, keepdims=True)
        hot = lane == idx
        cur = jnp.where(hot, -jnp.inf, cur)
        vals.append(m)
        idxs.append(idx)
        hots.append(hot)
    es = [jnp.exp(v - vals[0]) for v in vals]
    denom = es[0] + es[1] + es[2] + es[3]

    member = hots[0] | hots[1] | hots[2] | hots[3]
    mf = member.astype(F32)
    before = jnp.dot(ltri_ref[...], mf.astype(BF16), preferred_element_type=F32) + cnt_sc[...]
    cnt_sc[...] += jnp.sum(mf, axis=0, keepdims=True)
    cnt_ref[...] = cnt_sc[...]

    ti = jnp.zeros((tm, LANES), I32)
    tw = jnp.zeros((tm, LANES), F32)
    rk = jnp.zeros((tm, LANES), I32)
    for c in range(TOP_K):
        sel = lane_i == c
        rank = jnp.sum(jnp.where(hots[c], before, 0.0), axis=-1, keepdims=True)
        ti = jnp.where(sel, idxs[c].astype(I32), ti)
        tw = jnp.where(sel, es[c] / denom, tw)
        rk = jnp.where(sel, rank.astype(I32), rk)
    ti_ref[...] = ti
    tw_ref[...] = tw
    rk_ref[...] = rk


def _mixout(hp, atp, cop, hs, ats, cos, woa, woc, g, wr1, wr2, br):
    tm = TM_MIX
    np_blk = hp.shape[0] // tm
    ns_blk = hs.shape[0] // tm
    n_all = hp.shape[0] + hs.shape[0]
    ltri = jnp.tril(jnp.ones((tm, tm), F32), -1).astype(BF16)
    const = lambda i: (0, 0)
    row_p = lambda i: (jnp.minimum(i, np_blk - 1), 0)
    row_s = lambda i: (jnp.maximum(i - np_blk, 0), 0)
    row_out = lambda i: (i, 0)
    ins = [hp, atp, cop, hs, ats, cos, woa, woc, g, wr1, wr2, br, ltri]
    in_specs = [
        pl.BlockSpec((tm, D_MODEL), row_p),
        pl.BlockSpec((tm, ATTN_W), row_p),
        pl.BlockSpec((tm, CONV_W), row_p),
        pl.BlockSpec((tm, D_MODEL), row_s),
        pl.BlockSpec((tm, ATTN_W), row_s),
        pl.BlockSpec((tm, CONV_W), row_s),
        pl.BlockSpec(woa.shape, const),
        pl.BlockSpec(woc.shape, const),
        pl.BlockSpec(g.shape, const),
        pl.BlockSpec(wr1.shape, const),
        pl.BlockSpec(wr2.shape, const),
        pl.BlockSpec(br.shape, const),
        pl.BlockSpec(ltri.shape, const),
    ]
    out_shape = (
        jax.ShapeDtypeStruct((n_all, D_MODEL), F32),
        jax.ShapeDtypeStruct((n_all * ROW_SUB, LANES), F32),
        jax.ShapeDtypeStruct((n_all, LANES), I32),
        jax.ShapeDtypeStruct((n_all, LANES), F32),
        jax.ShapeDtypeStruct((n_all, LANES), I32),
    )
    outs = pl.pallas_call(
        functools.partial(_mixout_kernel, np_blk=np_blk),
        grid=(np_blk + ns_blk,),
        in_specs=in_specs,
        out_specs=(
            pl.BlockSpec((tm, D_MODEL), row_out),
            pl.BlockSpec((tm * ROW_SUB, LANES), row_out),
            pl.BlockSpec((tm, LANES), row_out),
            pl.BlockSpec((tm, LANES), row_out),
            pl.BlockSpec((tm, LANES), row_out),
            pl.BlockSpec((1, LANES), const),
        ),
        out_shape=out_shape + (jax.ShapeDtypeStruct((1, LANES), F32),),
        scratch_shapes=[pltpu.VMEM((1, LANES), F32)],
        compiler_params=_cparams(("arbitrary",)),
        name="mixout",
    )(*ins)
    return outs[:5], outs[5]


def _dispatch_kernel(pad_lo_ref, pad_n_ref, dst_ref, x_ref, xs_hbm, zero, sem, zsem):
    t = pl.program_id(0)
    tm = dst_ref.shape[2] // TOP_K

    def rows(ref, i):
        return ref.at[pl.ds(pl.multiple_of(i * ROW_SUB, ROW_SUB), ROW_SUB)]

    def zero_fill(e, wait):
        lo = pad_lo_ref[e]
        n_big = lax.shift_right_logical(pad_n_ref[e], ZERO_SLOTS.bit_length() - 1)
        n_one = pad_n_ref[e] & (ZERO_SLOTS - 1)

        def big(k, carry):
            at = 0 if wait else pl.multiple_of((lo + k * ZERO_SLOTS) * ROW_SUB, ROW_SUB)
            cp = pltpu.make_async_copy(zero, xs_hbm.at[pl.ds(at, ZERO_SLOTS * ROW_SUB)], zsem.at[0])
            cp.wait() if wait else cp.start()
            return carry

        def one(k, carry):
            cp = pltpu.make_async_copy(rows(zero, 0), rows(xs_hbm, 0 if wait else lo + n_big * ZERO_SLOTS + k),
                                       zsem.at[0])
            cp.wait() if wait else cp.start()
            return carry

        lax.fori_loop(0, n_big, big, 0)
        lax.fori_loop(0, n_one, one, 0)

    @pl.when(t == 0)
    def _():
        zero[...] = jnp.zeros_like(zero)
        for e in range(pad_n_ref.shape[0]):
            zero_fill(e, wait=False)

    def issue(r, carry):
        for c in range(TOP_K):
            pltpu.make_async_copy(rows(x_ref, r), rows(xs_hbm, dst_ref[0, 0, r * TOP_K + c]),
                                  sem.at[0]).start(priority=c % 2)
        return carry
    lax.fori_loop(0, tm, issue, 0, unroll=4)

    def drain(r, carry):
        for c in range(TOP_K):
            pltpu.make_async_copy(rows(x_ref, 0), rows(xs_hbm, 0), sem.at[0]).wait()
        return carry
    lax.fori_loop(0, tm, drain, 0, unroll=4)

    @pl.when(t == pl.num_programs(0) - 1)
    def _():
        for e in range(pad_n_ref.shape[0]):
            zero_fill(e, wait=True)


def _dispatch(pad_lo, pad_n, dest_blk, xn_all, n_slots):
    nblk = dest_blk.shape[0]
    return pl.pallas_call(
        _dispatch_kernel,
        grid_spec=pltpu.PrefetchScalarGridSpec(
            num_scalar_prefetch=2,
            grid=(nblk,),
            in_specs=[
                pl.BlockSpec((1, 1, dest_blk.shape[2]), lambda t, lo, n: (t, 0, 0), memory_space=pltpu.SMEM),
                pl.BlockSpec((dest_blk.shape[2] // TOP_K * ROW_SUB, LANES), lambda t, lo, n: (t, 0)),
            ],
            out_specs=pl.BlockSpec(memory_space=pl.ANY),
            scratch_shapes=[pltpu.VMEM((ZERO_SLOTS * ROW_SUB, LANES), F32),
                            pltpu.SemaphoreType.DMA((1,)),
                            pltpu.SemaphoreType.DMA((1,))],
        ),
        out_shape=jax.ShapeDtypeStruct((n_slots * ROW_SUB, LANES), F32),
        compiler_params=_cparams(("arbitrary",)),
        name="dispatch",
    )(pad_lo, pad_n, dest_blk, xn_all)


def _moe_kernel(te_ref, nu_ref, x_ref, wgu_ref, bgu_ref, wd_ref, bd_ref, y_ref, wgu_bf, wd_bf):
    t = pl.program_id(0)
    n_used = nu_ref[0]

    @pl.when(t >= n_used)
    def _():
        y_ref[...] = jnp.zeros_like(y_ref)

    @pl.when(t < n_used)
    def _():
        prev_e = te_ref[jnp.maximum(t - 1, 0)]

        @pl.when((t == 0) | (te_ref[t] != prev_e))
        def _():
            wgu_bf[...] = wgu_ref[...].astype(BF16)
            wd_bf[...] = wd_ref[...].astype(BF16)

        x = _load_rows(x_ref).astype(BF16)
        hh = jnp.dot(x, wgu_bf[...], preferred_element_type=F32) + bgu_ref[...]
        g = jnp.minimum(hh[:, :D_FF], SWIGLU_LIMIT)
        u = jnp.clip(hh[:, D_FF:], -SWIGLU_LIMIT, SWIGLU_LIMIT)
        act = (u + 1.0) * g * jax.nn.sigmoid(SWIGLU_ALPHA * g)
        _store_rows(y_ref, jnp.dot(act.astype(BF16), wd_bf[...], preferred_element_type=F32) + bd_ref[...])


def _moe(tile_expert, n_used, x_sorted, wgu, bgu, wd, bd):
    tm = TM_MOE
    n_tiles = x_sorted.shape[0] // (tm * ROW_SUB)
    e_map = lambda t, te, nu: (te[t], 0, 0)
    return pl.pallas_call(
        _moe_kernel,
        grid_spec=pltpu.PrefetchScalarGridSpec(
            num_scalar_prefetch=2,
            grid=(n_tiles,),
            in_specs=[
                pl.BlockSpec((tm * ROW_SUB, LANES), lambda t, te, nu: (jnp.minimum(t, nu[0] - 1), 0)),
                pl.BlockSpec((None, D_MODEL, 2 * D_FF), e_map),
                pl.BlockSpec((None, 1, 2 * D_FF), e_map),
                pl.BlockSpec((None, D_FF, D_MODEL), e_map),
                pl.BlockSpec((None, 1, D_MODEL), e_map),
            ],
            out_specs=pl.BlockSpec((tm * ROW_SUB, LANES), lambda t, te, nu: (t, 0)),
            scratch_shapes=[pltpu.VMEM((D_MODEL, 2 * D_FF), BF16),
                            pltpu.VMEM((D_FF, D_MODEL), BF16)],
        ),
        out_shape=jax.ShapeDtypeStruct(x_sorted.shape, F32),
        compiler_params=_cparams(("arbitrary",)),
        name="moe",
    )(tile_expert, n_used, x_sorted, wgu, bgu, wd, bd)


def _ple_kernel(dst_cur, dst_nxt, y_hbm, h1_ref, tw_ref, p_ref, g_ref, wg_ref, wp_ref, o_ref, ybuf, sem):
    t = pl.program_id(0)
    nt = pl.num_programs(0)
    slot = t % 2
    tm = o_ref.shape[0]

    def row_copy(row, c, r, s):
        return pltpu.make_async_copy(y_hbm.at[pl.ds(pl.multiple_of(row * ROW_SUB, ROW_SUB), ROW_SUB)],
                                     ybuf.at[s, c, pl.ds(pl.multiple_of(r * ROW_SUB, ROW_SUB), ROW_SUB)], sem.at[s])

    def gather(dst, s):
        def body(r, carry):
            for c in range(TOP_K):
                row_copy(dst[0, 0, r * TOP_K + c], c, r, s).start(priority=c % 2)
            return carry
        lax.fori_loop(0, tm, body, 0, unroll=4)

    def drain(s):
        def body(r, carry):
            for c in range(TOP_K):
                row_copy(0, c, r, s).wait()
            return carry
        lax.fori_loop(0, tm, body, 0, unroll=4)

    @pl.when(t == 0)
    def _():
        gather(dst_cur, 0)

    @pl.when(t + 1 < nt)
    def _():
        gather(dst_nxt, 1 - slot)

    drain(slot)
    tw = tw_ref[...]
    moe = tw[:, 0:1] * _load_rows(ybuf.at[slot, 0])
    for c in range(1, TOP_K):
        moe = moe + tw[:, c:c + 1] * _load_rows(ybuf.at[slot, c])
    h2 = h1_ref[...] + moe
    gate = jax.nn.sigmoid(jnp.dot(_rms(h2, g_ref[...]).astype(BF16), wg_ref[...], preferred_element_type=F32))
    proj = jnp.dot(p_ref[...].astype(BF16), wp_ref[...], preferred_element_type=F32)
    o_ref[...] = h2 + gate * proj


def _ple(dest, y_sorted, h1_all, tw_all, p, g, wg, wp, row_off):
    rows = p.shape[0]
    tm = min(TM_PLE, rows)
    nblk = rows // tm
    blk_off = row_off // tm
    last = nblk - 1
    const = lambda t: (0, 0)
    return pl.pallas_call(
        _ple_kernel,
        grid=(nblk,),
        in_specs=[
            pl.BlockSpec((1, 1, tm * TOP_K), lambda t: (t + blk_off, 0, 0), memory_space=pltpu.SMEM),
            pl.BlockSpec((1, 1, tm * TOP_K), lambda t: (jnp.minimum(t + 1, last) + blk_off, 0, 0),
                         memory_space=pltpu.SMEM),
            pl.BlockSpec(memory_space=pl.ANY),
            pl.BlockSpec((tm, D_MODEL), lambda t: (t + blk_off, 0)),
            pl.BlockSpec((tm, LANES), lambda t: (t + blk_off, 0)),
            pl.BlockSpec((tm, p.shape[1]), lambda t: (t, 0)),
            pl.BlockSpec(g.shape, const),
            pl.BlockSpec(wg.shape, const),
            pl.BlockSpec(wp.shape, const),
        ],
        out_specs=pl.BlockSpec((tm, D_MODEL), lambda t: (t, 0)),
        out_shape=jax.ShapeDtypeStruct((rows, D_MODEL), F32),
        scratch_shapes=[pltpu.VMEM((2, TOP_K, tm * ROW_SUB, LANES), F32), pltpu.SemaphoreType.DMA((2,))],
        compiler_params=_cparams(("arbitrary",)),
        name="ple_%d" % rows,
    )(dest, dest, y_sorted, h1_all, tw_all, p, g, wg, wp)


def kernel(x_prompt, x_sample, p_prompt, p_sample, cache_k, cache_v, state_conv, page_table, g_mix, w_in,
           q_norm_w, k_norm_w, sb_bias, conv_w, w_out, g_moe, w_router, b_router, w_gate_up, b_gate_up, w_down,
           b_down, g_ple, w_ple_gate, w_ple_proj):
    depth = g_mix.shape[0]
    assert depth == 1, "single-layer step"
    batch, seq, _ = x_prompt.shape
    dec_b, dec_seq, _ = x_sample.shape
    n_pool = cache_k.shape[1]
    n_p = batch * seq
    n_s = dec_b * dec_seq
    n_all = n_p + n_s
    assert seq % TM_IN == 0 and TM_IN % BK == 0 and n_p % TM_MIX == 0 and n_s % TM_MIX == 0
    assert seq % TQ_ATT == 0 and TQ_ATT % BK == 0
    assert n_p % TM_PLE == 0 and n_s % TM_PLE == 0
    assert page_table.shape[1] % PAGES_PER_STEP == 0 and cache_k.shape[2] == PAGE

    log2e = math.log2(math.e)
    scale2 = log2e / math.sqrt(HEAD_DIM)
    bias2 = sb_bias[0] * log2e
    w_in_bf = w_in[0].astype(BF16)
    head_id = jnp.arange(ATTN_W) // HEAD_DIM
    bd = jnp.where(head_id[:, None] == head_id[None, :], 1.0 / HEAD_DIM, 0.0).astype(BF16)
    qnw = (jnp.tile(q_norm_w[0], N_HEADS) * scale2)[None, :]
    knw = jnp.tile(k_norm_w[0], N_HEADS)[None, :]
    g_mix2 = g_mix[0][None, :]
    cw = conv_w[0]
    woa = w_out[0, :ATTN_W].astype(BF16)
    woc = w_out[0, ATTN_W:].astype(BF16)
    wr = jnp.pad(w_router[0], ((0, 0), (0, LANES - N_EXPERTS)))
    wr_hi = wr.astype(BF16)
    wr_lo = (wr - wr_hi.astype(F32)).astype(BF16)
    wr1 = jnp.concatenate([wr_hi, wr_lo], axis=1)
    br = jnp.concatenate([b_router[0], jnp.full((LANES - N_EXPERTS,), -1e30, F32)])[None, :]
    g_moe2 = g_moe[0][None, :]
    g_ple2 = g_ple[0][None, :]
    wpg = w_ple_gate[0].astype(BF16)
    wpp = w_ple_proj[0].astype(BF16)

    xp = x_prompt.reshape(n_p, D_MODEL)
    xs = x_sample.reshape(n_s, D_MODEL)
    qw_p, kt_p, vt_p, ktb_p, vtb_p, co_p, ut_p = _inproj_prompt(xp, g_mix2, w_in_bf, bd, qnw, knw, cw, batch, seq)
    st = state_conv[0]
    zero_s = jnp.zeros((dec_b, dec_seq, CONV_W), F32)
    halo1 = zero_s.at[:, 0].set(st[:, 1]).reshape(n_s, CONV_W)
    halo2 = zero_s.at[:, 0].set(st[:, 0]).at[:, 1].set(st[:, 1]).reshape(n_s, CONV_W)
    q_s, k_s, v_s, co_s, u_s = _inproj_sample(xs, g_mix2, w_in_bf, bd, qnw, knw, cw, halo1, halo2, dec_seq)

    u_tri = jnp.tril(jnp.ones((BK, BK), F32), -1).astype(BF16)
    attn_p = _attn_prompt(bias2, qw_p, ktb_p, vtb_p, u_tri, batch, seq)

    def new_page(a):
        at = jnp.swapaxes(a.reshape(dec_b, dec_seq, ATTN_W), 1, 2).astype(BF16)
        return jnp.pad(at, ((0, 0), (0, 0), (0, PAGE - dec_seq)))

    row_head = jnp.arange(N_HEADS * dec_seq) // dec_seq
    bias_rows = jnp.broadcast_to(bias2[row_head][:, None], (N_HEADS * dec_seq, PAGE)).astype(F32)
    bm = (row_head[:, None] == head_id[None, :]).astype(F32)
    ucat = jnp.concatenate([jnp.tril(jnp.ones((PAGE, PAGE), F32), -1), jnp.ones((PAGE, PAGE), F32)],
                           axis=1).astype(BF16)
    ck = jnp.transpose(cache_k[0], (0, 2, 3, 1))
    cv = jnp.transpose(cache_v[0], (0, 2, 3, 1))
    attn_s = _attn_sample(page_table, bias_rows, q_s.reshape(dec_b, dec_seq, ATTN_W), new_page(k_s), new_page(v_s),
                          ck, cv, ucat, bm, dec_seq)

    bufs, cnt = _mixout(xp, attn_p, co_p, xs, attn_s.reshape(n_s, ATTN_W).astype(BF16), co_s, woa, woc, g_moe2,
                        wr1, wr_hi, br)
    h1_all, xn_all, ti_all, tw_all, rk_all = bufs

    n_tiles = (n_all * TOP_K + N_EXPERTS * (TM_MOE - 1)) // TM_MOE + 1
    counts = cnt[0, :N_EXPERTS].astype(I32)
    padded = ((counts + TM_MOE - 1) // TM_MOE) * TM_MOE
    ends = jnp.cumsum(padded)
    starts = ends - padded
    top_i = ti_all[:, :TOP_K]
    dest = starts[top_i] + rk_all[:, :TOP_K]
    tile_start = jnp.arange(n_tiles, dtype=I32) * TM_MOE
    tile_expert = jnp.minimum(jnp.sum((tile_start[:, None] >= ends[None, :]).astype(I32), axis=1), N_EXPERTS - 1)
    n_used = (ends[-1] // TM_MOE).astype(I32)[None]
    dest_blk = dest.reshape(n_all // TM_PLE, 1, TM_PLE * TOP_K)

    n_slots = n_tiles * TM_MOE
    pad_lo = jnp.concatenate([starts + counts, ends[-1:]])
    pad_n = jnp.concatenate([padded - counts, n_slots - ends[-1:]])
    x_sorted = _dispatch(pad_lo, pad_n, dest_blk, xn_all, n_slots)
    y_sorted = _moe(tile_expert, n_used, x_sorted, w_gate_up[0], b_gate_up[0][:, None, :], w_down[0],
                    b_down[0][:, None, :])

    y_p = _ple(dest_blk, y_sorted, h1_all, tw_all, p_prompt[0].reshape(n_p, -1), g_ple2, wpg, wpp, 0)
    y_s = _ple(dest_blk, y_sorted, h1_all, tw_all, p_sample[0].reshape(n_s, -1), g_ple2, wpg, wpp, n_p)

    new_conv_p = ut_p[:, HALO - 2:, :]
    new_conv_s = u_s.reshape(dec_b, dec_seq, CONV_W)[:, dec_seq - 2:, :]
    return (
        y_p.reshape(batch, seq, D_MODEL),
        y_s.reshape(dec_b, dec_seq, D_MODEL),
        jnp.transpose(kt_p.reshape(batch, N_HEADS, HEAD_DIM, seq), (0, 3, 1, 2))[None],
        jnp.transpose(vt_p.reshape(batch, N_HEADS, HEAD_DIM, seq), (0, 3, 1, 2))[None],
        new_conv_p[None],
        k_s.reshape(1, dec_b, dec_seq, N_HEADS, HEAD_DIM),
        v_s.reshape(1, dec_b, dec_seq, N_HEADS, HEAD_DIM),
        new_conv_s[None],
    )
```
